```python
import jax, jax.numpy as jnp
from jax import lax
import numpy as np

D_MODEL = 1024
BATCH = 8
SEQ = 4096
DEPTH = 2

HEAD_DIM = 64
ROPE_THETA = 10000.0
NORM_EPS = 1e-6
N_BRANCH = 4
BRANCH_W = 4 * HEAD_DIM

A_HEADS = 4
A_BLOCK = 256
A_TOPK = 3
B_HEADS = 4
B_CMP_LEN = 32
B_CMP_STRIDE = 16
B_CMP_HIDDEN = 256
B_SLC_LEN = 64
B_SLC_TOPN = 16
B_WINDOW = 512
C_HEADS = 4
D_HEADS = 4
D_PATTERNS = ((128, 1), (512, 4), (2048, 16))

GATHER_QBLOCK = 32
DENSE_QBLOCK = 128

IN_SIZES = (
    BRANCH_W, BRANCH_W, BRANCH_W, BRANCH_W,
    BRANCH_W, 6 * HEAD_DIM, BRANCH_W, 3 * B_HEADS,
    3 * BRANCH_W, BRANCH_W,
    3 * len(D_PATTERNS) * D_HEADS * HEAD_DIM, BRANCH_W,
    N_BRANCH * D_MODEL,
)
IN_WIDTH = sum(IN_SIZES)

kernel_name = 'hybrid_moba_nsa_stickbreak_dilated'


def _rmsnorm(x, w):
    xf = x.astype(jnp.float32)
    y = xf * lax.rsqrt(jnp.mean(xf * xf, axis=-1, keepdims=True) + NORM_EPS)
    return (y * w.astype(jnp.float32)).astype(x.dtype)


def _rope(x, pos):
    half = HEAD_DIM // 2
    inv = ROPE_THETA ** (-jnp.arange(half, dtype=jnp.float32) / half)
    ang = pos.astype(jnp.float32)[:, None] * inv[None, :]
    cos = jnp.cos(ang).astype(x.dtype)
    sin = jnp.sin(ang).astype(x.dtype)
    x1, x2 = x[..., :half], x[..., half:]
    return jnp.concatenate([x1 * cos - x2 * sin, x1 * sin + x2 * cos], axis=-1)


def _heads(x, n):
    b, s = x.shape[:2]
    return x.reshape(b, s, n, HEAD_DIM).transpose(0, 2, 1, 3)


def _merge_heads(x):
    b, h, s, d = x.shape
    return x.transpose(0, 2, 1, 3).reshape(b, s, h * d)


def _masked_softmax(s, mask):
    s = jnp.where(mask, s, -jnp.inf)
    mx = jnp.max(s, axis=-1, keepdims=True)
    mx = jnp.where(jnp.isfinite(mx), mx, 0.0)
    e = jnp.where(mask, jnp.exp(s - mx), 0.0)
    den = jnp.maximum(jnp.sum(e, axis=-1, keepdims=True), 1e-30)
    return e / den, (mx + jnp.log(den))[..., 0]


def _unblock(y):
    nb, b, h, qb, d = y.shape
    return jnp.moveaxis(y, 0, 2).reshape(b, h, nb * qb, d)


def _banded_attention(q, k, v, window):
    b, h, L, d = q.shape
    hk = k.shape[1]
    g = h // hk
    qb = min(DENSE_QBLOCK, L)
    nblk = -(-L // qb)
    lp = nblk * qb
    q = jnp.pad(q, ((0, 0), (0, 0), (0, lp - L), (0, 0))).reshape(b, hk, g, lp, d)
    kpad = ((0, 0), (0, 0), (window, lp - L), (0, 0))
    k = jnp.pad(k, kpad)
    v = jnp.pad(v, kpad)
    span = window + qb
    scale = d ** -0.5

    def block(q0):
        qs = lax.dynamic_slice_in_dim(q, q0, qb, axis=3)
        ks = lax.dynamic_slice_in_dim(k, q0, span, axis=2)
        vs = lax.dynamic_slice_in_dim(v, q0, span, axis=2)
        s = jnp.einsum('bhgqd,bhkd->bhgqk', qs, ks, preferred_element_type=jnp.float32) * scale
        tq = q0 + jnp.arange(qb)
        tk = q0 - window + jnp.arange(span)
        dist = tq[:, None] - tk[None, :]
        mask = (dist >= 0) & (dist <= window) & (tk[None, :] >= 0)
        p, lse = _masked_softmax(s, mask)
        return jnp.einsum('bhgqk,bhkd->bhgqd', p.astype(vs.dtype), vs), lse

    o, lse = lax.map(block, jnp.arange(nblk) * qb)
    o = jnp.moveaxis(o, 0, 3).reshape(b, h, lp, d)[:, :, :L]
    lse = jnp.moveaxis(lse, 0, 3).reshape(b, h, lp)[:, :, :L]
    return o, lse


def _moba_attention(q, k, v):
    b, h, S, d = q.shape
    nb = -(-S // A_BLOCK)
    sp = nb * A_BLOCK
    pad = ((0, 0), (0, 0), (0, sp - S), (0, 0))
    k = jnp.pad(k, pad)
    v = jnp.pad(v, pad)
    kb = k.reshape(b, h, nb, A_BLOCK, d)
    vb = v.reshape(b, h, nb, A_BLOCK, d)
    k_mean = jnp.mean(kb.astype(jnp.float32), axis=3)
    gate = jnp.einsum('bhsd,bhnd->bhsn', q.astype(jnp.float32), k_mean)
    past = jnp.arange(nb)[None, :] < (jnp.arange(S) // A_BLOCK)[:, None]
    gate = jnp.where(past, gate, -jnp.inf)
    top_score, top_idx = lax.top_k(gate, min(A_TOPK, nb))
    top_ok = top_score > -jnp.inf
    topk = top_idx.shape[-1]
    n_sel = topk * A_BLOCK
    scale = d ** -0.5
    pick = jax.vmap(jax.vmap(lambda blocks, i: blocks[i]))

    def block(q0):
        qs = lax.dynamic_slice_in_dim(q, q0, GATHER_QBLOCK, axis=2)
        ids = lax.dynamic_slice_in_dim(top_idx, q0, GATHER_QBLOCK, axis=2)
        ok = lax.dynamic_slice_in_dim(top_ok, q0, GATHER_QBLOCK, axis=2)
        k_sel = pick(kb, ids).reshape(b, h, GATHER_QBLOCK, n_sel, d)
        v_sel = pick(vb, ids).reshape(b, h, GATHER_QBLOCK, n_sel, d)
        own0 = (q0 // A_BLOCK) * A_BLOCK
        k_own = lax.dynamic_slice_in_dim(k, own0, A_BLOCK, axis=2)
        v_own = lax.dynamic_slice_in_dim(v, own0, A_BLOCK, axis=2)
        tq = q0 + jnp.arange(GATHER_QBLOCK)
        tk = own0 + jnp.arange(A_BLOCK)
        s_sel = jnp.einsum('bhqd,bhqkd->bhqk', qs, k_sel, preferred_element_type=jnp.float32) * scale
        s_own = jnp.einsum('bhqd,bhkd->bhqk', qs, k_own, preferred_element_type=jnp.float32) * scale
        s = jnp.concatenate([s_sel, s_own], axis=-1)
        mask = jnp.concatenate([
            jnp.repeat(ok, A_BLOCK, axis=-1),
            jnp.broadcast_to(tk[None, :] <= tq[:, None], s_own.shape)], axis=-1)
        p, _ = _masked_softmax(s, mask)
        p = p.astype(v.dtype)
        return (jnp.einsum('bhqk,bhqkd->bhqd', p[..., :n_sel], v_sel)
                + jnp.einsum('bhqk,bhkd->bhqd', p[..., n_sel:], v_own))

    return _unblock(lax.map(block, jnp.arange(S // GATHER_QBLOCK) * GATHER_QBLOCK))


def _nsa_attention(q, kv, gates, cmp_pos, cmp_w1, cmp_w2, pos):
    b, h, S, d = q.shape
    scale = d ** -0.5
    k_c, v_c, k_s, v_s, k_w, v_w = [kv[:, :, i] for i in range(6)]
    q_rot = _rope(q, pos)
    k_s = _rope(k_s, pos)
    k_w = _rope(k_w, pos)

    nc = (S - B_CMP_LEN) // B_CMP_STRIDE + 1
    starts = np.arange(nc) * B_CMP_STRIDE
    gidx = starts[:, None] + np.arange(B_CMP_LEN)[None, :]

    def compress(x, pe, w1, w2):
        blocks = (x[:, gidx] + pe).reshape(b, nc, B_CMP_LEN * d)
        return jax.nn.gelu(blocks @ w1) @ w2

    k_cmp = compress(k_c, cmp_pos[0], cmp_w1[0], cmp_w2[0])
    v_cmp = compress(v_c, cmp_pos[1], cmp_w1[1], cmp_w2[1])
    s_cmp = jnp.einsum('bhsd,bnd->bhsn', q, k_cmp, preferred_element_type=jnp.float32) * scale
    vis = jnp.asarray(starts + B_CMP_LEN - 1)[None, :] <= pos[:, None]
    p_cmp, _ = _masked_softmax(s_cmp, vis)
    o_cmp = jnp.einsum('bhsn,bnd->bhsd', p_cmp.astype(v_cmp.dtype), v_cmp)

    nsel = S // B_SLC_LEN
    j = np.arange(nsel)
    overlap = ((starts[:, None] < (j[None, :] + 1) * B_SLC_LEN)
               & (starts[:, None] + B_CMP_LEN > j[None, :] * B_SLC_LEN)).astype(np.float32)
    imp = jnp.einsum('bhsn,nj->bsj', p_cmp, jnp.asarray(overlap))
    cur = (pos // B_SLC_LEN)[:, None]
    jj = jnp.arange(nsel)[None, :]
    forced = (jj == 0) | (jj == cur) | (jj == cur - 1)
    imp = jnp.where(jj <= cur, jnp.where(forced, jnp.inf, imp), -jnp.inf)
    top_score, top_idx = lax.top_k(imp, min(B_SLC_TOPN, nsel))
    top_ok = top_score > -jnp.inf
    topn = top_idx.shape[-1]
    n_key = topn * B_SLC_LEN
    ksb = k_s.reshape(b, nsel, B_SLC_LEN, d)
    vsb = v_s.reshape(b, nsel, B_SLC_LEN, d)
    pick = jax.vmap(lambda blocks, i: blocks[i])

    def block(q0):
        qs = lax.dynamic_slice_in_dim(q_rot, q0, GATHER_QBLOCK, axis=2)
        ids = lax.dynamic_slice_in_dim(top_idx, q0, GATHER_QBLOCK, axis=1)
        ok = lax.dynamic_slice_in_dim(top_ok, q0, GATHER_QBLOCK, axis=1)
        kg = pick(ksb, ids).reshape(b, GATHER_QBLOCK, n_key, d)
        vg = pick(vsb, ids).reshape(b, GATHER_QBLOCK, n_key, d)
        tk = (ids[..., None] * B_SLC_LEN + jnp.arange(B_SLC_LEN)).reshape(b, GATHER_QBLOCK, n_key)
        tq = q0 + jnp.arange(GATHER_QBLOCK)
        mask = jnp.repeat(ok, B_SLC_LEN, axis=-1) & (tk <= tq[None, :, None])
        s = jnp.einsum('bhqd,bqkd->bhqk', qs, kg, preferred_element_type=jnp.float32) * scale
        p, _ = _masked_softmax(s, mask[:, None])
        return jnp.einsum('bhqk,bqkd->bhqd', p.astype(vg.dtype), vg)

    o_slc = _unblock(lax.map(block, jnp.arange(S // GATHER_QBLOCK) * GATHER_QBLOCK))

    o_win, _ = _banded_attention(q_rot, k_w[:, None], v_w[:, None], B_WINDOW - 1)
    return gates[0] * o_cmp + gates[1] * o_slc + gates[2] * o_win


def _stick_breaking_attention(q, k, v):
    b, h, S, d = q.shape
    scale = d ** -0.5
    outs = []
    for i in range(S // DENSE_QBLOCK):
        q0, end = i * DENSE_QBLOCK, (i + 1) * DENSE_QBLOCK
        z = jnp.einsum('bhqd,bhkd->bhqk', q[:, :, q0:end], k[:, :, :end],
                       preferred_element_type=jnp.float32) * scale
        tq = q0 + jnp.arange(DENSE_QBLOCK)
        causal = jnp.arange(end)[None, :] < tq[:, None]
        log_1m = jnp.where(causal, jax.nn.log_sigmoid(-z), 0.0)
        tail = lax.cumsum(log_1m, axis=3, reverse=True) - log_1m
        a = jnp.where(causal, jnp.exp(jax.nn.log_sigmoid(z) + tail), 0.0)
        outs.append(jnp.einsum('bhqk,bhkd->bhqd', a.astype(v.dtype), v[:, :, :end]))
    return jnp.concatenate(outs, axis=2)


def _to_sub(x, dil):
    b, h, S, d = x.shape
    return x.reshape(b, h, S // dil, dil, d).transpose(0, 1, 3, 2, 4).reshape(b, h * dil, S // dil, d)


def _dilated_attention(q, k, v):
    b, _, S, d = q.shape
    outs, lses = [], []
    for g, (window, dil) in enumerate(D_PATTERNS):
        sl = slice(g * D_HEADS, (g + 1) * D_HEADS)
        L = S // dil
        o, lse = _banded_attention(_to_sub(q[:, sl], dil), _to_sub(k[:, sl], dil),
                                   _to_sub(v[:, sl], dil), window // dil)
        outs.append(o.reshape(b, D_HEADS, dil, L, d).transpose(0, 1, 3, 2, 4).reshape(b, D_HEADS, S, d))
        lses.append(lse.reshape(b, D_HEADS, dil, L).transpose(0, 1, 3, 2).reshape(b, D_HEADS, S))
    w = jax.nn.softmax(jnp.stack(lses), axis=0)
    return jnp.einsum('gbhs,gbhsd->bhsd', w.astype(q.dtype), jnp.stack(outs))


def _hybrid_layer(x, norm_w, w_in, cmp_pos, cmp_w1, cmp_w2, w_up, w_out):
    b, S, _ = x.shape
    pos = jnp.arange(S)
    h = _rmsnorm(x, norm_w)
    proj = jnp.einsum('bsd,dc->bsc', h, w_in)
    (qa, ka, va, ga, qb, kvb, gb, nsa_g, qkvc, gc, qkvd, gd, merge) = jnp.split(
        proj, np.cumsum(IN_SIZES)[:-1].tolist(), axis=-1)

    o_a = _moba_attention(_rope(_heads(qa, A_HEADS), pos), _rope(_heads(ka, A_HEADS), pos),
                          _heads(va, A_HEADS))

    nsa_gates = jax.nn.sigmoid(nsa_g.astype(jnp.float32)).reshape(b, S, 3, B_HEADS)
    nsa_gates = nsa_gates.transpose(2, 0, 3, 1)[..., None].astype(x.dtype)
    o_b = _nsa_attention(_heads(qb, B_HEADS), kvb.reshape(b, S, 6, HEAD_DIM), nsa_gates,
                         cmp_pos, cmp_w1, cmp_w2, pos)

    qc, kc, vc = jnp.split(qkvc, 3, axis=-1)
    o_c = _stick_breaking_attention(_heads(qc, C_HEADS), _heads(kc, C_HEADS), _heads(vc, C_HEADS))

    n_d = len(D_PATTERNS) * D_HEADS
    qd, kd, vd = jnp.split(qkvd, 3, axis=-1)
    o_d = _dilated_attention(_rope(_heads(qd, n_d), pos), _rope(_heads(kd, n_d), pos), _heads(vd, n_d))

    widened = jnp.stack([_merge_heads(o_a) * jax.nn.silu(ga), _merge_heads(o_b) * jax.nn.silu(gb),
                         _merge_heads(o_c) * jax.nn.silu(gc), _merge_heads(o_d) * jax.nn.silu(gd)],
                        axis=2)
    u = jnp.einsum('bsiw,iwd->bsid', widened, w_up)
    merge_g = jax.nn.sigmoid(merge.astype(jnp.float32)).astype(x.dtype).reshape(b, S, N_BRANCH, D_MODEL)
    y = jnp.sum(merge_g * u, axis=2)
    return x + y @ w_out


def setup_inputs(seed: int = 0) -> dict:
    key = jax.random.key(seed)
    ks = jax.random.split(key, 9)
    f32 = jnp.float32
    x = jax.random.normal(ks[0], (BATCH, SEQ, D_MODEL), f32)
    norm_w = 1.0 + 0.02 * jax.random.normal(ks[1], (DEPTH, D_MODEL), f32)
    w_in = jax.random.normal(ks[2], (DEPTH, D_MODEL, IN_WIDTH), f32) * D_MODEL ** -0.5
    nsa_cmp_pos = 0.02 * jax.random.normal(ks[3], (DEPTH, 2, B_CMP_LEN, HEAD_DIM), f32)
    nsa_cmp_w1 = jax.random.normal(ks[4], (DEPTH, 2, B_CMP_LEN * HEAD_DIM, B_CMP_HIDDEN), f32) * (B_CMP_LEN * HEAD_DIM) ** -0.5
    nsa_cmp_w2 = jax.random.normal(ks[5], (DEPTH, 2, B_CMP_HIDDEN, HEAD_DIM), f32) * B_CMP_HIDDEN ** -0.5
    w_up = jax.random.normal(ks[6], (DEPTH, N_BRANCH, BRANCH_W, D_MODEL), f32) * BRANCH_W ** -0.5
    w_out = jax.random.normal(ks[7], (DEPTH, D_MODEL, D_MODEL), f32) * D_MODEL ** -0.5
    final_norm_w = 1.0 + 0.02 * jax.random.normal(ks[8], (D_MODEL,), f32)
    return {'x': x, 'norm_w': norm_w, 'w_in': w_in, 'nsa_cmp_pos': nsa_cmp_pos,
            'nsa_cmp_w1': nsa_cmp_w1, 'nsa_cmp_w2': nsa_cmp_w2, 'w_up': w_up,
            'w_out': w_out, 'final_norm_w': final_norm_w}


def reference(x, norm_w, w_in, nsa_cmp_pos, nsa_cmp_w1, nsa_cmp_w2, w_up, w_out, final_norm_w):
    for layer in range(DEPTH):
        x = _hybrid_layer(x, norm_w[layer], w_in[layer], nsa_cmp_pos[layer], nsa_cmp_w1[layer],
                          nsa_cmp_w2[layer], w_up[layer], w_out[layer])
    return _rmsnorm(x, final_norm_w)
```

```python
import functools

import numpy as np
import jax
import jax.numpy as jnp
from jax import lax
from jax.experimental import pallas as pl
from jax.experimental.pallas import tpu as pltpu

F32 = jnp.float32
BF16 = jnp.bfloat16

HEAD_DIM = 64
N_HEADS = 4
BRANCH_W = N_HEADS * HEAD_DIM
ROPE_THETA = 10000.0
NORM_EPS = 1e-6
QK_SCALE = HEAD_DIM ** -0.5

A_BLOCK = 256
A_TOPK = 3
B_CMP_LEN = 32
B_CMP_STRIDE = 16
B_CMP_HIDDEN = 256
B_SLC_LEN = 64
B_SLC_TOPN = 16
B_WINDOW = 512
D_PATTERNS = ((128, 1), (512, 4), (2048, 16))
N_BRANCH = 4

IN_SIZES = (
    BRANCH_W, BRANCH_W, BRANCH_W, BRANCH_W,
    BRANCH_W, 6 * HEAD_DIM, BRANCH_W, 3 * N_HEADS,
    3 * BRANCH_W, BRANCH_W,
    3 * len(D_PATTERNS) * N_HEADS * HEAD_DIM, BRANCH_W,
    N_BRANCH * 1024,
)

NEG = -1e30
SB_DEAD = -120.0
VMEM_LIMIT = 56 * 1024 * 1024

TOK_TILE = 256
DENSE_TQ = 128


def _dot(a, b):
    return jnp.dot(a, b, preferred_element_type=F32)


def _dot_nt(a, b):
    return lax.dot_general(a, b, (((1,), (1,)), ((), ())), preferred_element_type=F32)


def _dot_tn(a, b):
    return lax.dot_general(a, b, (((0,), (0,)), ((), ())), preferred_element_type=F32)


def _params(*sem):
    return pltpu.CompilerParams(dimension_semantics=sem, vmem_limit_bytes=VMEM_LIMIT)


def _rmsnorm(x, w):
    y = x * lax.rsqrt(jnp.mean(x * x, axis=-1, keepdims=True) + NORM_EPS)
    return y * w


def _head_mask(shape):
    return lax.broadcasted_iota(jnp.int32, shape, 1) >> 6


def _keep_head(q, head, h):
    return jnp.where(head == h, q.astype(F32), 0.0).astype(BF16)


_C_QA, _C_KA, _C_VA, _C_QB = 0, 256, 512, 768
_C_KSKW, _C_KCVC, _C_VSVW = 1024, 1152, 1280
_C_QC, _C_KC, _C_VC = 1408, 1664, 1920
_C_QD, _C_KD, _C_VD = 2176, 2944, 3712
_W1_COLS = 4480


def _inproj_kernel(x_ref, nw_ref, w_ref, cos_ref, sin_ref,
                   qa_ref, ka_ref, va_ref, kmean_ref, qb_ref, qslc_ref, qwin_ref,
                   kskw_ref, kcvc_ref, vsvw_ref, qc_ref, kc_ref, vc_ref,
                   qd0_ref, qd1_ref, qd2_ref, kd0_ref, kd1_ref, kd2_ref,
                   vd0_ref, vd1_ref, vd2_ref):
    ts = x_ref.shape[1]
    h = _rmsnorm(x_ref[0], nw_ref[...]).astype(BF16)
    cos = cos_ref[...]
    sin = sin_ref[...]
    lane = lax.broadcasted_iota(jnp.int32, (ts, 256), 1)
    first = (lane & 63) < 32
    lane128 = lax.broadcasted_iota(jnp.int32, (ts, 128), 1)
    first128 = (lane128 & 63) < 32
    lo128 = lane128 < 64

    def proj(c0, w=256):
        return _dot(h, w_ref[:, c0:c0 + w])

    def rope(v):
        partner = jnp.where(first, pltpu.roll(v, 224, 1), pltpu.roll(v, 32, 1))
        return v * cos + partner * sin

    qa_ref[0] = (rope(proj(_C_QA)) * QK_SCALE).astype(BF16)
    ka = rope(proj(_C_KA))
    ka_ref[0] = ka.astype(BF16)
    kmean_ref[0, 0] = jnp.sum(ka, axis=0, keepdims=True) * (1.0 / ts)
    va_ref[0] = proj(_C_VA).astype(BF16)

    qb = proj(_C_QB)
    qb_ref[0] = (qb * QK_SCALE).astype(BF16)
    qbr = rope(qb) * QK_SCALE
    zero = jnp.zeros((ts, 128), F32)
    for pair in range(2):
        q01 = qbr[:, pair * 128:(pair + 1) * 128]
        r01 = pltpu.roll(q01, 64, 1)
        qslc_ref[0, 2 * pair] = jnp.where(lo128, q01, zero).astype(BF16)
        qslc_ref[0, 2 * pair + 1] = jnp.where(lo128, r01, zero).astype(BF16)
        qwin_ref[0, 2 * pair] = jnp.where(lo128, zero, r01).astype(BF16)
        qwin_ref[0, 2 * pair + 1] = jnp.where(lo128, zero, q01).astype(BF16)

    kskw = proj(_C_KSKW, 128)
    partner = jnp.where(first128, pltpu.roll(kskw, 96, 1), pltpu.roll(kskw, 32, 1))
    kskw_ref[0] = (kskw * cos[:, :128] + partner * sin[:, :128]).astype(BF16)
    kcvc_ref[0] = proj(_C_KCVC, 128)
    vsvw_ref[0] = proj(_C_VSVW, 128).astype(BF16)

    qc_ref[0] = (proj(_C_QC) * QK_SCALE).astype(BF16)
    kc_ref[0] = proj(_C_KC).astype(BF16)
    vc_ref[0] = proj(_C_VC).astype(BF16)

    for g, (q_ref, k_ref, v_ref) in enumerate(((qd0_ref, kd0_ref, vd0_ref),
                                               (qd1_ref, kd1_ref, vd1_ref),
                                               (qd2_ref, kd2_ref, vd2_ref))):
        q_ref[0] = (rope(proj(_C_QD + 256 * g)) * QK_SCALE).astype(BF16)
        k_ref[0] = rope(proj(_C_KD + 256 * g)).astype(BF16)
        v_ref[0] = proj(_C_VD + 256 * g).astype(BF16)


def _inproj(x, norm_w, w1, cos_t, sin_t):
    b, s, d = x.shape
    ts = TOK_TILE
    nt = s // ts
    tok = lambda w, dt: jax.ShapeDtypeStruct((b, s, w), dt)
    tok_spec = lambda w: pl.BlockSpec((1, ts, w), lambda i, j: (i, j, 0))
    head_spec = pl.BlockSpec((1, N_HEADS, ts, 128), lambda i, j: (i, 0, j, 0))
    out_shape = [
        tok(256, BF16), tok(256, BF16), tok(256, BF16),
        jax.ShapeDtypeStruct((b, nt, 1, 256), F32),
        tok(256, BF16),
        jax.ShapeDtypeStruct((b, N_HEADS, s, 128), BF16),
        jax.ShapeDtypeStruct((b, N_HEADS, s, 128), BF16),
        tok(128, BF16), tok(128, F32), tok(128, BF16),
        tok(256, BF16), tok(256, BF16), tok(256, BF16),
    ] + [tok(256, BF16)] * 9
    out_specs = [
        tok_spec(256), tok_spec(256), tok_spec(256),
        pl.BlockSpec((1, 1, 1, 256), lambda i, j: (i, j, 0, 0)),
        tok_spec(256), head_spec, head_spec,
        tok_spec(128), tok_spec(128), tok_spec(128),
        tok_spec(256), tok_spec(256), tok_spec(256),
    ] + [tok_spec(256)] * 9
    return pl.pallas_call(
        _inproj_kernel,
        grid=(b, nt),
        in_specs=[
            pl.BlockSpec((1, ts, d), lambda i, j: (i, j, 0)),
            pl.BlockSpec((1, d), lambda i, j: (0, 0)),
            pl.BlockSpec((d, _W1_COLS), lambda i, j: (0, 0)),
            pl.BlockSpec((ts, 256), lambda i, j: (j, 0)),
            pl.BlockSpec((ts, 256), lambda i, j: (j, 0)),
        ],
        out_specs=out_specs,
        out_shape=out_shape,
        compiler_params=_params("parallel", "arbitrary"),
        name="inproj",
    )(x, norm_w, w1, cos_t, sin_t)


def _softmax_step(s, allowed, v, m_ref, l_ref, acc_ref):
    s = jnp.where(allowed, s, NEG)
    m_old = m_ref[...]
    m_new = jnp.maximum(m_old, jnp.max(s, axis=1, keepdims=True))
    p = jnp.where(allowed, jnp.exp(s - m_new), 0.0)
    alpha = jnp.exp(m_old - m_new)
    l_ref[...] = alpha * l_ref[...] + jnp.sum(p, axis=1, keepdims=True)
    acc_ref[...] = alpha * acc_ref[...] + _dot(p.astype(BF16), v)
    m_ref[...] = m_new


def _softmax_init(m_ref, l_ref, acc_ref):
    m_ref[...] = jnp.full(m_ref.shape, NEG, F32)
    l_ref[...] = jnp.zeros(l_ref.shape, F32)
    acc_ref[...] = jnp.zeros(acc_ref.shape, F32)


def _moba_kernel(q_ref, k_ref, v_ref, km_ref, o_ref, m_ref, l_ref, acc_ref, *, nb, topk):
    tq = q_ref.shape[1]
    tk = tq
    qb = pl.program_id(1)
    q = q_ref[0]
    head = _head_mask((tq, 256))
    kmean = km_ref[0].astype(BF16)
    blk = lax.broadcasted_iota(jnp.int32, (tq, nb), 1)
    past = blk < qb
    rows = lax.broadcasted_iota(jnp.int32, (tq, tk), 0)
    cols = lax.broadcasted_iota(jnp.int32, (tq, tk), 1)
    causal = cols <= rows
    out = jnp.zeros((tq, 256), F32)
    for h in range(N_HEADS):
        hm = head == h
        qh = _keep_head(q, head, h)
        gate = jnp.where(past, _dot_nt(qh, kmean), -jnp.inf)
        rank = jnp.zeros((tq, nb), jnp.int32)
        for i in range(nb):
            col = gate[:, i:i + 1]
            beats = (col > gate) | ((col == gate) & (blk > i))
            rank = rank + jnp.where(beats, 1, 0)
        sel = jnp.where((rank < topk) & past, 1.0, 0.0)
        _softmax_init(m_ref, l_ref, acc_ref)

        def body(j, carry, qh=qh, sel=sel):
            off = pl.multiple_of(j * tk, tk)
            selcol = jnp.max(jnp.where(blk == j, sel, 0.0), axis=1, keepdims=True)
            allowed = jnp.broadcast_to(selcol > 0.5, (tq, tk))
            s = _dot_nt(qh, k_ref[0, pl.ds(off, tk), :])
            _softmax_step(s, allowed, v_ref[0, pl.ds(off, tk), :], m_ref, l_ref, acc_ref)
            return carry

        lax.fori_loop(0, qb, body, 0)
        off = pl.multiple_of(qb * tk, tk)
        s = _dot_nt(qh, k_ref[0, pl.ds(off, tk), :])
        _softmax_step(s, causal, v_ref[0, pl.ds(off, tk), :], m_ref, l_ref, acc_ref)
        out = jnp.where(hm, acc_ref[...] / l_ref[...], out)
    o_ref[0] = out


def _moba(qa, ka, va, kmean):
    b, s, _ = qa.shape
    tq = A_BLOCK
    nb = s // A_BLOCK
    kern = functools.partial(_moba_kernel, nb=nb, topk=min(A_TOPK, nb))
    seq = pl.BlockSpec((1, s, 256), lambda i, j: (i, 0, 0))
    return pl.pallas_call(
        kern,
        grid=(b, nb),
        in_specs=[pl.BlockSpec((1, tq, 256), lambda i, j: (i, j, 0)), seq, seq,
                  pl.BlockSpec((1, nb, 256), lambda i, j: (i, 0, 0))],
        out_specs=pl.BlockSpec((1, tq, 256), lambda i, j: (i, j, 0)),
        out_shape=jax.ShapeDtypeStruct((b, s, 256), F32),
        scratch_shapes=[pltpu.VMEM((tq, 1), F32), pltpu.VMEM((tq, 1), F32),
                        pltpu.VMEM((tq, 256), F32)],
        compiler_params=_params("parallel", "arbitrary"),
        name="moba",
    )(qa, ka, va, kmean)


def _gelu_tanh(x):
    return 0.5 * x * (1.0 + jnp.tanh(np.sqrt(2.0 / np.pi).astype(np.float32) * (x + 0.044715 * (x * x * x))))


def _cmp_kernel(xr_ref, pelo_ref, pehi_ref, wlo_ref, whi_ref, w2k_ref, w2v_ref, q_ref, ovt_ref,
                o_ref, selt_ref, kbig_ref, vbig_ref, *, nsel, topn):
    tq = q_ref.shape[1]
    nr = xr_ref.shape[1]
    nc = nr - 1
    qi = pl.program_id(1)

    @pl.when(qi == 0)
    def _():
        xr = xr_ref[0]
        lo = (xr + pelo_ref[...]).astype(BF16)
        hi = (pltpu.roll(xr, nr - 1, 0) + pehi_ref[...]).astype(BF16)
        hid = _dot(lo, wlo_ref[...]) + _dot(hi, whi_ref[...])
        act = _gelu_tanh(hid).astype(BF16)
        kbig_ref[...] = _dot(act, w2k_ref[...]).astype(BF16)
        vbig_ref[...] = _dot(act, w2v_ref[...]).astype(BF16)

    q = q_ref[0]
    head = _head_mask((tq, 256))
    tpos = qi * tq + lax.broadcasted_iota(jnp.int32, (tq, nr), 0)
    n = lax.broadcasted_iota(jnp.int32, (tq, nr), 1)
    vis = (n * B_CMP_STRIDE + (B_CMP_LEN - 1) <= tpos) & (n < nc)
    ovt = ovt_ref[...]
    out = jnp.zeros((tq, 256), F32)
    imp = jnp.zeros((nsel, tq), F32)
    for h in range(N_HEADS):
        qh = _keep_head(q, head, h)
        s = jnp.where(vis, _dot_nt(qh, kbig_ref[:, h * 256:(h + 1) * 256]), NEG)
        mx = jnp.max(s, axis=1, keepdims=True)
        e = jnp.where(vis, jnp.exp(s - mx), 0.0)
        den = jnp.maximum(jnp.sum(e, axis=1, keepdims=True), 1e-30)
        p = (e / den).astype(BF16)
        out = out + _dot(p, vbig_ref[:, h * 256:(h + 1) * 256])
        imp = imp + _dot_nt(ovt, p)
    o_ref[0] = out

    jj = lax.broadcasted_iota(jnp.int32, (nsel, tq), 0)
    cur = (qi * tq + lax.broadcasted_iota(jnp.int32, (nsel, tq), 1)) >> 6
    forced = (jj == 0) | (jj == cur) | (jj == cur - 1)
    valid = jj <= cur
    imp = jnp.where(valid, jnp.where(forced, jnp.inf, imp), -jnp.inf)
    rank = jnp.zeros((nsel, tq), jnp.int32)
    for i in range(nsel):
        row = imp[i:i + 1, :]
        beats = (row > imp) | ((row == imp) & (jj > i))
        rank = rank + jnp.where(beats, 1, 0)
    selt_ref[0] = jnp.where((rank < topn) & valid, 1.0, 0.0).astype(BF16)


def _nsa_cmp(kcvc, qb, pe_lo, pe_hi, w_lo, w_hi, w2k, w2v, ovt):
    b, s, _ = qb.shape
    nr = s // B_CMP_STRIDE
    nsel = s // B_SLC_LEN
    tq = TOK_TILE
    xr = kcvc.reshape(b, nr, B_CMP_STRIDE * 128)
    kern = functools.partial(_cmp_kernel, nsel=nsel, topn=min(B_SLC_TOPN, nsel))
    full = lambda a: pl.BlockSpec(a.shape, lambda i, j: (0,) * a.ndim)
    return pl.pallas_call(
        kern,
        grid=(b, s // tq),
        in_specs=[pl.BlockSpec((1, nr, B_CMP_STRIDE * 128), lambda i, j: (i, 0, 0)),
                  full(pe_lo), full(pe_hi), full(w_lo), full(w_hi), full(w2k), full(w2v),
                  pl.BlockSpec((1, tq, 256), lambda i, j: (i, j, 0)), full(ovt)],
        out_specs=[pl.BlockSpec((1, tq, 256), lambda i, j: (i, j, 0)),
                   pl.BlockSpec((1, nsel, tq), lambda i, j: (i, 0, j))],
        out_shape=[jax.ShapeDtypeStruct((b, s, 256), F32),
                   jax.ShapeDtypeStruct((b, nsel, s), BF16)],
        scratch_shapes=[pltpu.VMEM((nr, 1024), BF16), pltpu.VMEM((nr, 1024), BF16)],
        compiler_params=_params("parallel", "arbitrary"),
        name="nsa_cmp",
    )(xr, pe_lo, pe_hi, w_lo, w_hi, w2k, w2v, qb, ovt)


def _pair_lanes(a, b, upper):
    lo = lax.broadcasted_iota(jnp.int32, a.shape, 1) < 64
    if upper:
        return jnp.where(lo, pltpu.roll(a, 64, 1), b)
    return jnp.where(lo, a, pltpu.roll(b, 64, 1))


def _nsa_kernel(qs_ref, qw_ref, kk_ref, vv_ref, selt_ref, oslc_ref, owin_ref,
                m_ref, l_ref, acc_ref, *, nsel):
    tq = qs_ref.shape[2]
    rows4 = N_HEADS * tq
    qi = pl.program_id(1)
    q0 = qi * tq

    def finish(o_ref, upper):
        o = acc_ref[...] / jnp.maximum(l_ref[...], 1e-30)
        parts = [o[h * tq:(h + 1) * tq] for h in range(N_HEADS)]
        o_ref[0, :, 0:128] = _pair_lanes(parts[0], parts[1], upper)
        o_ref[0, :, 128:256] = _pair_lanes(parts[2], parts[3], upper)

    tk = 256
    qs = qs_ref[0].reshape(rows4, 128)
    selt = selt_ref[0]
    rows = lax.broadcasted_iota(jnp.int32, (tq, tk), 0)
    cols = lax.broadcasted_iota(jnp.int32, (tq, tk), 1)
    eb = lax.broadcasted_iota(jnp.int32, (nsel, tk), 0)
    ek = lax.broadcasted_iota(jnp.int32, (nsel, tk), 1)
    _softmax_init(m_ref, l_ref, acc_ref)

    def slc_body(j, carry):
        off = pl.multiple_of(j * tk, tk)
        expand = jnp.where(eb == ((off + ek) >> 6), 1.0, 0.0).astype(BF16)
        picked = _dot_tn(selt, expand)
        ok = jnp.where((picked > 0.5) & (off + cols <= q0 + rows), 1.0, 0.0)
        allowed = jnp.concatenate([ok] * N_HEADS, axis=0) > 0.5
        s = _dot_nt(qs, kk_ref[0, pl.ds(off, tk), :])
        _softmax_step(s, allowed, vv_ref[0, pl.ds(off, tk), :], m_ref, l_ref, acc_ref)
        return carry

    lax.fori_loop(0, (q0 + tq + tk - 1) // tk, slc_body, 0)
    finish(oslc_ref, upper=False)

    tk = tq
    qw = qw_ref[0].reshape(rows4, 128)
    rows = lax.broadcasted_iota(jnp.int32, (tq, tk), 0)
    cols = lax.broadcasted_iota(jnp.int32, (tq, tk), 1)
    _softmax_init(m_ref, l_ref, acc_ref)

    def win_body(j, carry):
        off = pl.multiple_of(j * tk, tk)
        dist = (q0 + rows) - (off + cols)
        ok = jnp.where((dist >= 0) & (dist < B_WINDOW), 1.0, 0.0)
        allowed = jnp.concatenate([ok] * N_HEADS, axis=0) > 0.5
        s = _dot_nt(qw, kk_ref[0, pl.ds(off, tk), :])
        _softmax_step(s, allowed, vv_ref[0, pl.ds(off, tk), :], m_ref, l_ref, acc_ref)
        return carry

    lax.fori_loop(jnp.maximum(qi - (B_WINDOW // tk), 0), qi + 1, win_body, 0)
    finish(owin_ref, upper=True)


def _nsa_slc_win(qslc, qwin, kskw, vsvw, selt):
    b, _, s, _ = qslc.shape
    tq = DENSE_TQ
    nsel = s // B_SLC_LEN
    kern = functools.partial(_nsa_kernel, nsel=nsel)
    qspec = pl.BlockSpec((1, N_HEADS, tq, 128), lambda i, j: (i, 0, j, 0))
    seq = pl.BlockSpec((1, s, 128), lambda i, j: (i, 0, 0))
    ospec = pl.BlockSpec((1, tq, 256), lambda i, j: (i, j, 0))
    return pl.pallas_call(
        kern,
        grid=(b, s // tq),
        in_specs=[qspec, qspec, seq, seq, pl.BlockSpec((1, nsel, tq), lambda i, j: (i, 0, j))],
        out_specs=[ospec, ospec],
        out_shape=[jax.ShapeDtypeStruct((b, s, 256), F32)] * 2,
        scratch_shapes=[pltpu.VMEM((N_HEADS * tq, 1), F32), pltpu.VMEM((N_HEADS * tq, 1), F32),
                        pltpu.VMEM((N_HEADS * tq, 128), F32)],
        compiler_params=_params("parallel", "arbitrary"),
        name="nsa_slc_win",
    )(qslc, qwin, kskw, vsvw, selt)


def _sb_kernel(q_ref, k_ref, v_ref, o_ref, c_ref, acc_ref):
    tq = q_ref.shape[1]
    tk = tq
    qi = pl.program_id(1)
    q = q_ref[0]
    head = _head_mask((tq, 256))
    rows = lax.broadcasted_iota(jnp.int32, (tq, tk), 0)
    cols = lax.broadcasted_iota(jnp.int32, (tq, tk), 1)
    after = jnp.where(rows > cols, 1.0, 0.0).astype(BF16)
    out = jnp.zeros((tq, 256), F32)
    for h in range(N_HEADS):
        hm = head == h
        qh = _keep_head(q, head, h)
        c_ref[...] = jnp.zeros(c_ref.shape, F32)
        acc_ref[...] = jnp.zeros(acc_ref.shape, F32)

        def cond(carry):
            j, cmax = carry
            return (j >= 0) & (cmax > SB_DEAD)

        def body(carry, qh=qh):
            j, _ = carry
            off = pl.multiple_of(j * tk, tk)
            z = _dot_nt(qh, k_ref[0, pl.ds(off, tk), :])
            causal = (off + cols) < (qi * tq + rows)
            sp = jnp.log1p(jnp.exp(-jnp.abs(z)))
            log_beta = -(jnp.maximum(-z, 0.0) + sp)
            log_1m = jnp.where(causal, -(jnp.maximum(z, 0.0) + sp), 0.0)
            hi = log_1m.astype(BF16)
            lo = (log_1m - hi.astype(F32)).astype(BF16)
            tail = _dot(hi, after) + _dot(lo, after)
            c = c_ref[...]
            a = jnp.where(causal, jnp.exp(log_beta + (tail + c)), 0.0)
            acc_ref[...] += _dot(a.astype(BF16), v_ref[0, pl.ds(off, tk), :])
            c_new = c + jnp.sum(log_1m, axis=1, keepdims=True)
            c_ref[...] = c_new
            return j - 1, jnp.max(c_new)

        lax.while_loop(cond, body, (qi, jnp.float32(0.0)))
        out = jnp.where(hm, acc_ref[...], out)
    o_ref[0] = out


def _stickbreak(qc, kc, vc):
    b, s, _ = qc.shape
    tq = DENSE_TQ
    seq = pl.BlockSpec((1, s, 256), lambda i, j: (i, 0, 0))
    tile = pl.BlockSpec((1, tq, 256), lambda i, j: (i, j, 0))
    return pl.pallas_call(
        _sb_kernel,
        grid=(b, s // tq),
        in_specs=[tile, seq, seq],
        out_specs=tile,
        out_shape=jax.ShapeDtypeStruct((b, s, 256), F32),
        scratch_shapes=[pltpu.VMEM((tq, 1), F32), pltpu.VMEM((tq, 256), F32)],
        compiler_params=_params("parallel", "arbitrary"),
        name="stickbreak",
    )(qc, kc, vc)


def _dil_kernel(q_ref, k_ref, v_ref, o_ref, lse_ref, m_ref, l_ref, acc_ref, *, window):
    tq = q_ref.shape[1]
    tk = tq
    qi = pl.program_id(1)
    q = q_ref[0]
    head = _head_mask((tq, 256))
    rows = lax.broadcasted_iota(jnp.int32, (tq, tk), 0)
    cols = lax.broadcasted_iota(jnp.int32, (tq, tk), 1)
    out = jnp.zeros((tq, 256), F32)
    lse = jnp.zeros((tq, 256), F32)
    for h in range(N_HEADS):
        hm = head == h
        qh = _keep_head(q, head, h)
        _softmax_init(m_ref, l_ref, acc_ref)

        def step(j, qh=qh):
            off = pl.multiple_of(j * tk, tk)
            dist = (qi * tq + rows) - (off + cols)
            allowed = (dist >= 0) & (dist <= window)
            s = _dot_nt(qh, k_ref[0, pl.ds(off, tk), :])
            _softmax_step(s, allowed, v_ref[0, pl.ds(off, tk), :], m_ref, l_ref, acc_ref)

        def body(j, carry):
            step(j)
            return carry

        lax.fori_loop(jnp.maximum(qi - window // tk, 0), qi + 1, body, 0)
        out = jnp.where(hm, acc_ref[...] / l_ref[...], out)
        lse = jnp.where(hm, m_ref[...] + jnp.log(l_ref[...]), lse)
    o_ref[0] = out
    lse_ref[0] = lse


def _dilated_group(q, k, v, window):
    n, L, _ = q.shape
    tq = DENSE_TQ
    kern = functools.partial(_dil_kernel, window=window)
    seq = pl.BlockSpec((1, L, 256), lambda i, j: (i, 0, 0))
    tile = pl.BlockSpec((1, tq, 256), lambda i, j: (i, j, 0))
    return pl.pallas_call(
        kern,
        grid=(n, L // tq),
        in_specs=[tile, seq, seq],
        out_specs=[tile, tile],
        out_shape=[jax.ShapeDtypeStruct((n, L, 256), F32)] * 2,
        scratch_shapes=[pltpu.VMEM((tq, 1), F32), pltpu.VMEM((tq, 1), F32),
                        pltpu.VMEM((tq, 256), F32)],
        compiler_params=_params("parallel", "arbitrary"),
        name="dilated",
    )(q, k, v)


def _to_sub(x, dil):
    b, s, w = x.shape
    if dil == 1:
        return x
    return x.reshape(b, s // dil, dil, w).transpose(0, 2, 1, 3).reshape(b * dil, s // dil, w)


def _from_sub(x, dil, b):
    if dil == 1:
        return x
    _, L, w = x.shape
    return x.reshape(b, dil, L, w).transpose(0, 2, 1, 3).reshape(b, L * dil, w)


_C_GATE, _C_MERGE, _C_NSAG = 0, 1024, 5120
_W2_COLS = 5888


def _out_kernel(x_ref, nw_ref, w2_ref, oa_ref, ocmp_ref, oslc_ref, owin_ref, oc_ref,
                od0_ref, od1_ref, od2_ref, ls0_ref, ls1_ref, ls2_ref,
                wup_ref, wout_ref, fnw_ref, y_ref, acc_ref, *, final):
    x = x_ref[0]
    h = _rmsnorm(x, nw_ref[...]).astype(BF16)

    def proj(c0, w):
        return _dot(h, w2_ref[:, c0:c0 + w])

    ng = jax.nn.sigmoid(proj(_C_NSAG, 768))
    o_b = ng[:, 0:256] * ocmp_ref[0] + ng[:, 256:512] * oslc_ref[0] + ng[:, 512:768] * owin_ref[0]

    ls0, ls1, ls2 = ls0_ref[0], ls1_ref[0], ls2_ref[0]
    mx = jnp.maximum(jnp.maximum(ls0, ls1), ls2)
    e0, e1, e2 = jnp.exp(ls0 - mx), jnp.exp(ls1 - mx), jnp.exp(ls2 - mx)
    den = e0 + e1 + e2
    o_d = (e0 / den) * od0_ref[0] + (e1 / den) * od1_ref[0] + (e2 / den) * od2_ref[0]

    branches = (oa_ref[0], o_b, oc_ref[0], o_d)
    acc_ref[...] = jnp.zeros(acc_ref.shape, F32)
    for i in range(N_BRANCH):
        g = proj(_C_GATE + 256 * i, 256)
        wide = (branches[i] * (g * jax.nn.sigmoid(g))).astype(BF16)
        for half in range(2):
            c0 = 512 * half
            u = _dot(wide, wup_ref[i, :, c0:c0 + 512])
            mg = jax.nn.sigmoid(proj(_C_MERGE + 1024 * i + c0, 512))
            acc_ref[:, c0:c0 + 512] += mg * u
    out = x + _dot(acc_ref[...].astype(BF16), wout_ref[...])
    if final:
        out = _rmsnorm(out, fnw_ref[...])
    y_ref[0] = out


def _merge_out(x, norm_w, w2, branch_outs, w_up, w_out, final_norm_w, final):
    b, s, d = x.shape
    ts = TOK_TILE
    tile256 = pl.BlockSpec((1, ts, 256), lambda i, j: (i, j, 0))
    const = lambda a: pl.BlockSpec(a.shape, lambda i, j: (0,) * a.ndim)
    kern = functools.partial(_out_kernel, final=final)
    return pl.pallas_call(
        kern,
        grid=(b, s // ts),
        in_specs=[pl.BlockSpec((1, ts, d), lambda i, j: (i, j, 0)), const(norm_w), const(w2)]
        + [tile256] * 11 + [const(w_up), const(w_out), const(final_norm_w)],
        out_specs=pl.BlockSpec((1, ts, d), lambda i, j: (i, j, 0)),
        out_shape=jax.ShapeDtypeStruct((b, s, d), F32),
        scratch_shapes=[pltpu.VMEM((ts, d), F32)],
        compiler_params=_params("parallel", "arbitrary"),
        name="merge_out",
    )(x, norm_w, w2, *branch_outs, w_up, w_out, final_norm_w)


def _rope_tables(s):
    half = HEAD_DIM // 2
    inv = ROPE_THETA ** (-jnp.arange(half, dtype=F32) / half)
    ang = jnp.arange(s).astype(F32)[:, None] * inv[None, :]
    cos, sin = jnp.cos(ang), jnp.sin(ang)
    cos_t = jnp.tile(jnp.concatenate([cos, cos], axis=-1), (1, N_HEADS))
    sin_t = jnp.tile(jnp.concatenate([-sin, sin], axis=-1), (1, N_HEADS))
    return cos_t, sin_t


def _layer_weights(w_in, cmp_pos, cmp_w1, cmp_w2):
    o = np.concatenate([[0], np.cumsum(IN_SIZES)])
    qa, ka, va, ga, qb, kvb, gb, nsag, qkvc, gc, qkvd, gd, merge = [int(v) for v in o[:-1]]
    cols = lambda a, w: w_in[:, a:a + w]
    hd = HEAD_DIM
    w1 = jnp.concatenate([
        cols(qa, 256), cols(ka, 256), cols(va, 256), cols(qb, 256),
        cols(kvb + 2 * hd, hd), cols(kvb + 4 * hd, hd),
        cols(kvb, 2 * hd),
        cols(kvb + 3 * hd, hd), cols(kvb + 5 * hd, hd),
        cols(qkvc, 768), cols(qkvd, 2304)], axis=1).astype(BF16)
    gate_cols = jnp.concatenate([cols(ga, 256), cols(gb, 256), cols(gc, 256), cols(gd, 256)], axis=1)
    nsag_cols = jnp.repeat(cols(nsag, 3 * N_HEADS), HEAD_DIM, axis=1)
    w2 = jnp.concatenate([gate_cols, cols(merge, 4096), nsag_cols], axis=1).astype(BF16)

    half = B_CMP_STRIDE
    zpe = jnp.zeros((B_CMP_LEN, hd), F32)
    pe = jnp.concatenate([cmp_pos[0], cmp_pos[1]], axis=-1)
    pe_lo = pe[:half].reshape(1, half * 128)
    pe_hi = pe[half:].reshape(1, half * 128)
    w1k = cmp_w1[0].reshape(B_CMP_LEN, hd, B_CMP_HIDDEN)
    w1v = cmp_w1[1].reshape(B_CMP_LEN, hd, B_CMP_HIDDEN)
    zw = jnp.zeros_like(w1k)
    w1_big = jnp.concatenate([jnp.concatenate([w1k, zw], axis=2),
                              jnp.concatenate([zw, w1v], axis=2)], axis=1)
    w_lo = w1_big[:half].reshape(half * 128, 2 * B_CMP_HIDDEN).astype(BF16)
    w_hi = w1_big[half:].reshape(half * 128, 2 * B_CMP_HIDDEN).astype(BF16)
    del zpe

    def placed(w2h, top):
        blocks = []
        for h in range(N_HEADS):
            blk = jnp.zeros((2 * B_CMP_HIDDEN, 256), F32)
            r0 = 0 if top else B_CMP_HIDDEN
            blk = blk.at[r0:r0 + B_CMP_HIDDEN, h * hd:(h + 1) * hd].set(w2h)
            blocks.append(blk)
        return jnp.concatenate(blocks, axis=1).astype(BF16)

    return w1, w2, pe_lo, pe_hi, w_lo, w_hi, placed(cmp_w2[0], True), placed(cmp_w2[1], False)


def _overlap_t(s):
    nr = s // B_CMP_STRIDE
    nc = nr - 1
    nsel = s // B_SLC_LEN
    starts = np.arange(nr) * B_CMP_STRIDE
    j = np.arange(nsel)
    ov = ((starts[None, :] < (j[:, None] + 1) * B_SLC_LEN)
          & (starts[None, :] + B_CMP_LEN > j[:, None] * B_SLC_LEN)
          & (np.arange(nr)[None, :] < nc))
    return jnp.asarray(ov.astype(np.float32)).astype(BF16)


def _layer(x, norm_w, w_in, cmp_pos, cmp_w1, cmp_w2, w_up, w_out, final_norm_w, final, tables, ovt):
    b, s, _ = x.shape
    w1, w2, pe_lo, pe_hi, w_lo, w_hi, w2k, w2v = _layer_weights(w_in, cmp_pos, cmp_w1, cmp_w2)
    nw = norm_w.reshape(1, -1)
    (qa, ka, va, kmean, qb, qslc, qwin, kskw, kcvc, vsvw, qc, kc, vc,
     qd0, qd1, qd2, kd0, kd1, kd2, vd0, vd1, vd2) = _inproj(x, nw, w1, *tables)

    o_a = _moba(qa, ka, va, kmean.reshape(b, s // A_BLOCK, 256))
    o_cmp, selt = _nsa_cmp(kcvc, qb, pe_lo, pe_hi, w_lo, w_hi, w2k, w2v, ovt)
    o_slc, o_win = _nsa_slc_win(qslc, qwin, kskw, vsvw, selt)
    o_c = _stickbreak(qc, kc, vc)

    od, ls = [], []
    for (window, dil), q, k, v in zip(D_PATTERNS, (qd0, qd1, qd2), (kd0, kd1, kd2), (vd0, vd1, vd2)):
        o, l = _dilated_group(_to_sub(q, dil), _to_sub(k, dil), _to_sub(v, dil), window // dil)
        od.append(_from_sub(o, dil, b))
        ls.append(_from_sub(l, dil, b))

    return _merge_out(x, nw, w2, (o_a, o_cmp, o_slc, o_win, o_c, *od, *ls),
                      w_up.astype(BF16), w_out.astype(BF16), final_norm_w.reshape(1, -1), final)


def kernel(x, norm_w, w_in, nsa_cmp_pos, nsa_cmp_w1, nsa_cmp_w2, w_up, w_out, final_norm_w):
    depth = norm_w.shape[0]
    s = x.shape[1]
    tables = _rope_tables(s)
    ovt = _overlap_t(s)
    for layer in range(depth):
        x = _layer(x, norm_w[layer], w_in[layer], nsa_cmp_pos[layer], nsa_cmp_w1[layer],
                   nsa_cmp_w2[layer], w_up[layer], w_out[layer], final_norm_w,
                   layer == depth - 1, tables, ovt)
    return x
```

```python
import functools

import numpy as np
import jax
import jax.numpy as jnp
from jax import lax
from jax.experimental import pallas as pl
from jax.experimental.pallas import tpu as pltpu

F32 = jnp.float32
BF16 = jnp.bfloat16

HEAD_DIM = 64
N_HEADS = 4
BRANCH_W = N_HEADS * HEAD_DIM
ROPE_THETA = 10000.0
NORM_EPS = 1e-6
QK_SCALE = HEAD_DIM ** -0.5

A_BLOCK = 256
A_SHIFT = 8
A_TOPK = 3
B_CMP_LEN = 32
B_CMP_STRIDE = 16
B_CMP_HIDDEN = 256
B_SLC_LEN = 64
B_SLC_SHIFT = 6
B_SLC_TOPN = 16
B_WINDOW = 512
D_PATTERNS = ((128, 1), (512, 4), (2048, 16))
N_BRANCH = 4

IN_SIZES = (
    BRANCH_W, BRANCH_W, BRANCH_W, BRANCH_W,
    BRANCH_W, 6 * HEAD_DIM, BRANCH_W, 3 * N_HEADS,
    3 * BRANCH_W, BRANCH_W,
    3 * len(D_PATTERNS) * N_HEADS * HEAD_DIM, BRANCH_W,
    N_BRANCH * 1024,
)

M_INIT = -1e30
MASKED = -2e30
SB_DEAD = -120.0
VMEM_LIMIT = 56 * 1024 * 1024
LANES = 128

TOK_TILE = 256
DENSE_TQ = 128


def _dot(a, b):
    return jnp.dot(a, b, preferred_element_type=F32)


def _dot_nt(a, b):
    return lax.dot_general(a, b, (((1,), (1,)), ((), ())), preferred_element_type=F32)


def _params(*sem):
    return pltpu.CompilerParams(dimension_semantics=sem, vmem_limit_bytes=VMEM_LIMIT)


def _rmsnorm(x, w):
    y = x * lax.rsqrt(jnp.mean(x * x, axis=-1, keepdims=True) + NORM_EPS)
    return y * w


def _head_mask(shape):
    return lax.broadcasted_iota(jnp.int32, shape, 1) >> 6


def _keep_head(q, head, h):
    return jnp.where(head == h, q.astype(F32), 0.0).astype(BF16)


def _widen(x, width):
    return x if width == LANES else jnp.concatenate([x] * (width // LANES), axis=1)


def _pair_lanes(a, b, upper):
    lo = lax.broadcasted_iota(jnp.int32, a.shape, 1) < 64
    if upper:
        return jnp.where(lo, pltpu.roll(a, 64, 1), b)
    return jnp.where(lo, a, pltpu.roll(b, 64, 1))


def _pad_rows_t(x_t, rows_before):
    n, cols = x_t.shape
    parts = [jnp.zeros((rows_before, cols), F32), x_t]
    rest = LANES - rows_before - n
    if rest:
        parts.append(jnp.zeros((rest, cols), F32))
    return jnp.transpose(jnp.concatenate(parts, axis=0))


def _flash_init(m_ref, l_ref, acc_ref):
    m_ref[...] = jnp.full(m_ref.shape, M_INIT, F32)
    l_ref[...] = jnp.zeros(l_ref.shape, F32)
    acc_ref[...] = jnp.zeros(acc_ref.shape, F32)


def _flash_step(qs, ks, vs, keep, m_ref, l_ref, acc_ref):
    scores = [_dot_nt(q, k) for q, k in zip(qs, ks)]
    for h, (s, v) in enumerate(zip(scores, vs)):
        if keep is not None:
            s = jnp.where(keep, s, MASKED)
        m_old = m_ref[h]
        m_new = jnp.maximum(m_old, jnp.max(s, axis=1, keepdims=True))
        p = jnp.exp(s - _widen(m_new, s.shape[1]))
        alpha = jnp.exp(m_old - m_new)
        l_ref[h] = alpha * l_ref[h] + jnp.sum(p, axis=1, keepdims=True)
        acc_ref[h] = _widen(alpha, acc_ref.shape[2]) * acc_ref[h] + _dot(p.astype(BF16), v)
        m_ref[h] = m_new


def _softmax_once(qs, ks, vs, keep):
    scores = [jnp.where(keep, _dot_nt(q, k), MASKED) for q, k in zip(qs, ks)]
    outs, maxes, sums = [], [], []
    for s, v in zip(scores, vs):
        m = jnp.max(s, axis=1, keepdims=True)
        p = jnp.exp(s - m)
        l = jnp.sum(p, axis=1, keepdims=True)
        outs.append(_dot(p.astype(BF16), v) / l)
        maxes.append(m)
        sums.append(l)
    return outs, maxes, sums


_C_QA, _C_KA, _C_VA, _C_QB = 0, 256, 512, 768
_C_KSKW, _C_KCVC, _C_VSVW = 1024, 1152, 1280
_C_QC, _C_KC, _C_VC = 1408, 1664, 1920
_C_QD, _C_KD, _C_VD = 2176, 2944, 3712
_W1_COLS = 4480


def _inproj_kernel(x_ref, nw_ref, w_ref, cos_ref, sin_ref,
                   qa_ref, ka_ref, va_ref, kmean_ref,
                   qb_ref, qn_ref, ksa_ref, kwp_ref, vsvw_ref, kcvc_ref,
                   qc_ref, kc_ref, vc_ref,
                   qd0_ref, kd0_ref, vd0_ref, qd1_ref, kd1_ref, vd1_ref, qd2_ref, kd2_ref, vd2_ref,
                   sub_ref):
    ts = x_ref.shape[1]
    pos0 = pl.program_id(1) * ts
    h = _rmsnorm(x_ref[0], nw_ref[...]).astype(BF16)
    cos = cos_ref[...]
    sin = sin_ref[...]
    lane = lax.broadcasted_iota(jnp.int32, (ts, 256), 1)
    first = (lane & 63) < 32
    lane128 = lax.broadcasted_iota(jnp.int32, (ts, 128), 1)
    pos128 = pos0 + lax.broadcasted_iota(jnp.int32, (ts, 128), 0)
    first128 = (lane128 & 63) < 32
    lo128 = lane128 < 64

    def proj(c0, w=256):
        return _dot(h, w_ref[:, c0:c0 + w])

    def rope(v):
        partner = jnp.where(first, pltpu.roll(v, 224, 1), pltpu.roll(v, 32, 1))
        return v * cos + partner * sin

    def split_heads(v):
        out = []
        for pair in range(2):
            p = v[:, pair * 128:(pair + 1) * 128]
            out.append(jnp.where(lo128, p, 0.0))
            out.append(jnp.where(lo128, pltpu.roll(p, 64, 1), 0.0))
        return out

    qa = split_heads(rope(proj(_C_QA)) * QK_SCALE)
    ka = split_heads(rope(proj(_C_KA)))
    va = split_heads(proj(_C_VA))
    block_hot = jnp.where(lane128 - 64 == (pos128 >> A_SHIFT), 1.0, 0.0)
    for hd in range(N_HEADS):
        qa_ref[0, hd] = qa[hd].astype(BF16)
        ka_ref[0, hd] = (ka[hd] + block_hot).astype(BF16)
        va_ref[0, hd] = va[hd].astype(BF16)
        kmean_ref[0, 0, pl.ds(hd, 1), :] = jnp.sum(ka[hd], axis=0, keepdims=True) * (1.0 / ts)

    qb = proj(_C_QB)
    qb_ref[0] = (qb * QK_SCALE).astype(BF16)
    qn = split_heads(rope(qb) * QK_SCALE)
    for hd in range(N_HEADS):
        qn_ref[0, hd] = qn[hd].astype(BF16)
    kskw = proj(_C_KSKW, 128)
    partner = jnp.where(first128, pltpu.roll(kskw, 96, 1), pltpu.roll(kskw, 32, 1))
    kskw = kskw * cos[:, :128] + partner * sin[:, :128]
    slc_hot = jnp.where(lane128 - 64 == (pos128 >> B_SLC_SHIFT), 1.0, 0.0)
    ksa_ref[0] = jnp.where(lo128, kskw, slc_hot).astype(BF16)
    kwp_ref[0] = jnp.where(lo128, pltpu.roll(kskw, 64, 1), 0.0).astype(BF16)
    vsvw_ref[0] = proj(_C_VSVW, 128).astype(BF16)
    kcvc_ref[0] = proj(_C_KCVC, 128)

    qc_ref[0] = (proj(_C_QC) * QK_SCALE).astype(BF16)
    kc_ref[0] = proj(_C_KC).astype(BF16)
    vc_ref[0] = proj(_C_VC).astype(BF16)

    def store_sub(ref, v, dil):
        if dil == 1:
            ref[0, 0] = v.astype(BF16)
            return
        n = ts // dil
        for half in range(2):
            sub_ref[half] = v[:, half * 128:(half + 1) * 128]
        for c in range(dil):
            for half in range(2):
                ref[0, c, :, half * 128:(half + 1) * 128] = (
                    sub_ref[half, pl.ds(c, n, stride=dil), :].astype(BF16))

    groups = ((qd0_ref, kd0_ref, vd0_ref), (qd1_ref, kd1_ref, vd1_ref), (qd2_ref, kd2_ref, vd2_ref))
    for g, (q_ref, k_ref, v_ref) in enumerate(groups):
        dil = D_PATTERNS[g][1]
        store_sub(q_ref, rope(proj(_C_QD + 256 * g)) * QK_SCALE, dil)
        store_sub(k_ref, rope(proj(_C_KD + 256 * g)), dil)
        store_sub(v_ref, proj(_C_VD + 256 * g), dil)


def _inproj(x, norm_w, w1, cos_t, sin_t):
    b, s, d = x.shape
    ts = TOK_TILE
    nt = s // ts
    tok = lambda w, dt: jax.ShapeDtypeStruct((b, s, w), dt)
    tok_spec = lambda w: pl.BlockSpec((1, ts, w), lambda i, j: (i, j, 0))
    head = jax.ShapeDtypeStruct((b, N_HEADS, s, 128), BF16)
    head_spec = pl.BlockSpec((1, N_HEADS, ts, 128), lambda i, j: (i, 0, j, 0))
    out_shape = [head, head, head, jax.ShapeDtypeStruct((b, nt, N_HEADS, 128), F32),
                 tok(256, BF16), head, tok(128, BF16), tok(128, BF16), tok(128, BF16), tok(128, F32),
                 tok(256, BF16), tok(256, BF16), tok(256, BF16)]
    out_specs = [head_spec, head_spec, head_spec,
                 pl.BlockSpec((1, 1, N_HEADS, 128), lambda i, j: (i, j, 0, 0)),
                 tok_spec(256), head_spec, tok_spec(128), tok_spec(128), tok_spec(128), tok_spec(128),
                 tok_spec(256), tok_spec(256), tok_spec(256)]
    for _, dil in D_PATTERNS:
        out_shape += [jax.ShapeDtypeStruct((b, dil, s // dil, 256), BF16)] * 3
        out_specs += [pl.BlockSpec((1, dil, ts // dil, 256), lambda i, j: (i, 0, j, 0))] * 3
    return pl.pallas_call(
        _inproj_kernel,
        grid=(b, nt),
        in_specs=[
            pl.BlockSpec((1, ts, d), lambda i, j: (i, j, 0)),
            pl.BlockSpec((1, d), lambda i, j: (0, 0)),
            pl.BlockSpec((d, _W1_COLS), lambda i, j: (0, 0)),
            pl.BlockSpec((ts, 256), lambda i, j: (j, 0)),
            pl.BlockSpec((ts, 256), lambda i, j: (j, 0)),
        ],
        out_specs=out_specs,
        out_shape=out_shape,
        scratch_shapes=[pltpu.VMEM((2, ts, 128), F32)],
        compiler_params=_params("parallel", "arbitrary"),
        name="inproj",
    )(x, norm_w, w1, cos_t, sin_t)


def _moba_kernel(q_ref, k_ref, v_ref, km_ref, o_ref, qc_ref, m_ref, l_ref, acc_ref, *, nb, topk):
    tq = q_ref.shape[2]
    tk = tq
    qb = pl.program_id(1)
    blk_t = lax.broadcasted_iota(jnp.int32, (nb, tq), 0)
    past_t = blk_t < qb
    rows = lax.broadcasted_iota(jnp.int32, (tq, tk), 0)
    cols = lax.broadcasted_iota(jnp.int32, (tq, tk), 1)
    causal = cols <= rows

    for h in range(N_HEADS):
        q = q_ref[0, h]
        gate = jnp.where(past_t, _dot_nt(km_ref[0, h].astype(BF16), q), -jnp.inf)
        rank = jnp.zeros((nb, tq), jnp.int32)
        for i in range(nb):
            row = gate[i:i + 1, :]
            beats = (row > gate) | ((row == gate) & (blk_t > i))
            rank = rank + jnp.where(beats, 1, 0)
        keep = ((rank < topk) & past_t) | (blk_t == qb)
        bias_t = jnp.where(keep, 0.0, MASKED)
        qc_ref[h] = (q.astype(F32) + _pad_rows_t(bias_t, 64)).astype(BF16)

    _flash_init(m_ref, l_ref, acc_ref)

    def step(j, keep):
        off = pl.multiple_of(j * tk, tk)
        _flash_step([qc_ref[h] for h in range(N_HEADS)],
                    [k_ref[0, h, pl.ds(off, tk), :] for h in range(N_HEADS)],
                    [v_ref[0, h, pl.ds(off, tk), :] for h in range(N_HEADS)],
                    keep, m_ref, l_ref, acc_ref)

    def body(j, carry):
        step(j, None)
        return carry

    lax.fori_loop(0, qb, body, 0)
    step(qb, causal)
    o = [acc_ref[h] / l_ref[h] for h in range(N_HEADS)]
    o_ref[0, :, 0:128] = _pair_lanes(o[0], o[1], upper=False)
    o_ref[0, :, 128:256] = _pair_lanes(o[2], o[3], upper=False)


def _moba(qa, ka, va, kmean):
    b, _, s, _ = qa.shape
    tq = A_BLOCK
    nb = s // A_BLOCK
    kern = functools.partial(_moba_kernel, nb=nb, topk=min(A_TOPK, nb))
    seq = pl.BlockSpec((1, N_HEADS, s, 128), lambda i, j: (i, 0, 0, 0))
    state = pltpu.VMEM((N_HEADS, tq, LANES), F32)
    return pl.pallas_call(
        kern,
        grid=(b, nb),
        in_specs=[pl.BlockSpec((1, N_HEADS, tq, 128), lambda i, j: (i, 0, j, 0)), seq, seq,
                  pl.BlockSpec((1, N_HEADS, nb, 128), lambda i, j: (i, 0, 0, 0))],
        out_specs=pl.BlockSpec((1, tq, 256), lambda i, j: (i, j, 0)),
        out_shape=jax.ShapeDtypeStruct((b, s, 256), F32),
        scratch_shapes=[pltpu.VMEM((N_HEADS, tq, 128), BF16), state, state, state],
        compiler_params=_params("parallel", "arbitrary"),
        name="moba",
    )(qa, ka, va, kmean)


def _gelu_tanh(x):
    return 0.5 * x * (1.0 + jnp.tanh(np.sqrt(2.0 / np.pi).astype(np.float32) * (x + 0.044715 * (x * x * x))))


def _cmp_kernel(xr_ref, pelo_ref, pehi_ref, wlo_ref, whi_ref, w2k_ref, w2v_ref, q_ref, ovt_ref,
                o_ref, qbias_ref, kbig_ref, vbig_ref, *, nsel, topn):
    tq = q_ref.shape[1]
    nr = xr_ref.shape[1]
    nc = nr - 1
    qi = pl.program_id(1)

    @pl.when(qi == 0)
    def _():
        xr = xr_ref[0]
        lo = (xr + pelo_ref[...]).astype(BF16)
        hi = (pltpu.roll(xr, nr - 1, 0) + pehi_ref[...]).astype(BF16)
        hid = _dot(lo, wlo_ref[...]) + _dot(hi, whi_ref[...])
        act = _gelu_tanh(hid).astype(BF16)
        kbig_ref[...] = _dot(act, w2k_ref[...]).astype(BF16)
        vbig_ref[...] = _dot(act, w2v_ref[...]).astype(BF16)

    q = q_ref[0]
    head = _head_mask((tq, 256))
    tpos = qi * tq + lax.broadcasted_iota(jnp.int32, (tq, nr), 0)
    n = lax.broadcasted_iota(jnp.int32, (tq, nr), 1)
    vis = (n * B_CMP_STRIDE + (B_CMP_LEN - 1) <= tpos) & (n < nc)
    ovt = ovt_ref[...]
    out = jnp.zeros((tq, 256), F32)
    imp = jnp.zeros((nsel, tq), F32)
    for h in range(N_HEADS):
        qh = _keep_head(q, head, h)
        s = jnp.where(vis, _dot_nt(qh, kbig_ref[:, h * 256:(h + 1) * 256]), MASKED)
        mx = jnp.max(s, axis=1, keepdims=True)
        e = jnp.where(vis, jnp.exp(s - mx), 0.0)
        den = jnp.maximum(jnp.sum(e, axis=1, keepdims=True), 1e-30)
        p = (e / den).astype(BF16)
        out = out + _dot(p, vbig_ref[:, h * 256:(h + 1) * 256])
        imp = imp + _dot_nt(ovt, p)
    o_ref[0] = out

    jj = lax.broadcasted_iota(jnp.int32, (nsel, tq), 0)
    cur = (qi * tq + lax.broadcasted_iota(jnp.int32, (nsel, tq), 1)) >> B_SLC_SHIFT
    forced = (jj == 0) | (jj == cur) | (jj == cur - 1)
    valid = jj <= cur
    imp = jnp.where(valid, jnp.where(forced, jnp.inf, imp), -jnp.inf)
    rank = jnp.zeros((nsel, tq), jnp.int32)
    for i in range(nsel):
        row = imp[i:i + 1, :]
        beats = (row > imp) | ((row == imp) & (jj > i))
        rank = rank + jnp.where(beats, 1, 0)
    bias_t = jnp.where((rank < topn) & valid, 0.0, MASKED)
    qbias_ref[0] = _pad_rows_t(bias_t, 64).astype(BF16)


def _nsa_cmp(kcvc, qb, pe_lo, pe_hi, w_lo, w_hi, w2k, w2v, ovt):
    b, s, _ = qb.shape
    nr = s // B_CMP_STRIDE
    nsel = s // B_SLC_LEN
    assert nsel <= 64, "the selection mask lives in 64 spare contraction lanes"
    tq = TOK_TILE
    xr = kcvc.reshape(b, nr, B_CMP_STRIDE * 128)
    kern = functools.partial(_cmp_kernel, nsel=nsel, topn=min(B_SLC_TOPN, nsel))
    full = lambda a: pl.BlockSpec(a.shape, lambda i, j: (0,) * a.ndim)
    return pl.pallas_call(
        kern,
        grid=(b, s // tq),
        in_specs=[pl.BlockSpec((1, nr, B_CMP_STRIDE * 128), lambda i, j: (i, 0, 0)),
                  full(pe_lo), full(pe_hi), full(w_lo), full(w_hi), full(w2k), full(w2v),
                  pl.BlockSpec((1, tq, 256), lambda i, j: (i, j, 0)), full(ovt)],
        out_specs=[pl.BlockSpec((1, tq, 256), lambda i, j: (i, j, 0)),
                   pl.BlockSpec((1, tq, 128), lambda i, j: (i, j, 0))],
        out_shape=[jax.ShapeDtypeStruct((b, s, 256), F32),
                   jax.ShapeDtypeStruct((b, s, 128), BF16)],
        scratch_shapes=[pltpu.VMEM((nr, 1024), BF16), pltpu.VMEM((nr, 1024), BF16)],
        compiler_params=_params("parallel", "arbitrary"),
        name="nsa_cmp",
    )(xr, pe_lo, pe_hi, w_lo, w_hi, w2k, w2v, qb, ovt)


def _nsa_kernel(qn_ref, qbias_ref, ksa_ref, kwp_ref, vv_ref, oslc_ref, owin_ref,
                m_ref, l_ref, acc_ref, *, span):
    tq = qn_ref.shape[2]
    qi = pl.program_id(1)
    q0 = qi * tq

    def finish(o_ref, o, upper):
        o_ref[0, :, 0:128] = _pair_lanes(o[0], o[1], upper)
        o_ref[0, :, 128:256] = _pair_lanes(o[2], o[3], upper)

    def step(q, k_ref, off, tk, keep):
        kt = k_ref[0, pl.ds(off, tk), :]
        vt = vv_ref[0, pl.ds(off, tk), :]
        _flash_step(q, [kt] * N_HEADS, [vt] * N_HEADS, keep, m_ref, l_ref, acc_ref)

    qn = [qn_ref[0, h] for h in range(N_HEADS)]

    tk = 2 * tq
    bias = qbias_ref[0].astype(F32)
    qs = [(qn[h].astype(F32) + bias).astype(BF16) for h in range(N_HEADS)]
    jd = q0 // tk
    rows = lax.broadcasted_iota(jnp.int32, (tq, tk), 0)
    cols = lax.broadcasted_iota(jnp.int32, (tq, tk), 1)
    causal = cols <= rows + (q0 - jd * tk)
    _flash_init(m_ref, l_ref, acc_ref)

    def slc_body(j, carry):
        step(qs, ksa_ref, pl.multiple_of(j * tk, tk), tk, None)
        return carry

    lax.fori_loop(0, jd, slc_body, 0)
    step(qs, ksa_ref, pl.multiple_of(jd * tk, tk), tk, causal)
    finish(oslc_ref, [acc_ref[h] / jnp.maximum(l_ref[h], 1e-30) for h in range(N_HEADS)], upper=False)

    k0 = pl.multiple_of(jnp.maximum(q0 + tq - span, 0), tq)
    rows = lax.broadcasted_iota(jnp.int32, (tq, span), 0)
    cols = lax.broadcasted_iota(jnp.int32, (tq, span), 1)
    dist = (q0 + rows) - (k0 + cols)
    kt = kwp_ref[0, pl.ds(k0, span), :]
    vt = vv_ref[0, pl.ds(k0, span), :]
    o, _, _ = _softmax_once(qn, [kt] * N_HEADS, [vt] * N_HEADS, (dist >= 0) & (dist < B_WINDOW))
    finish(owin_ref, o, upper=True)


def _nsa_slc_win(qn, qbias, ksa, kwp, vsvw):
    b, _, s, _ = qn.shape
    tq = DENSE_TQ
    seq = pl.BlockSpec((1, s, 128), lambda i, j: (i, 0, 0))
    ospec = pl.BlockSpec((1, tq, 256), lambda i, j: (i, j, 0))
    state = pltpu.VMEM((N_HEADS, tq, LANES), F32)
    return pl.pallas_call(
        functools.partial(_nsa_kernel, span=min(B_WINDOW + tq, s)),
        grid=(b, s // tq),
        in_specs=[pl.BlockSpec((1, N_HEADS, tq, 128), lambda i, j: (i, 0, j, 0)),
                  pl.BlockSpec((1, tq, 128), lambda i, j: (i, j, 0)), seq, seq, seq],
        out_specs=[ospec, ospec],
        out_shape=[jax.ShapeDtypeStruct((b, s, 256), F32)] * 2,
        scratch_shapes=[state, state, state],
        compiler_params=_params("parallel", "arbitrary"),
        name="nsa_slc_win",
    )(qn, qbias, ksa, kwp, vsvw)


def _sb_kernel(q_ref, k_ref, v_ref, o_ref, c_ref, acc_ref):
    tq = q_ref.shape[1]
    tk = tq
    qi = pl.program_id(1)
    q = q_ref[0]
    head = _head_mask((tq, 256))
    qh = [_keep_head(q, head, h) for h in range(N_HEADS)]
    rows = lax.broadcasted_iota(jnp.int32, (tq, tk), 0)
    cols = lax.broadcasted_iota(jnp.int32, (tq, tk), 1)
    after = jnp.where(rows > cols, 1.0, 0.0).astype(BF16)
    after2 = jnp.concatenate([after, after], axis=0)
    c_ref[...] = jnp.zeros(c_ref.shape, F32)
    acc_ref[...] = jnp.zeros(acc_ref.shape, F32)

    def cond(carry):
        j, cmax = carry
        return (j >= 0) & (cmax > SB_DEAD)

    def body(carry):
        j, _ = carry
        off = pl.multiple_of(j * tk, tk)
        kt = k_ref[0, pl.ds(off, tk), :]
        vt = v_ref[0, pl.ds(off, tk), :]
        causal = (off + cols) < (qi * tq + rows)
        zs = [_dot_nt(qh[h], kt) for h in range(N_HEADS)]
        log_betas, log_1ms, tails = [], [], []
        for z in zs:
            sp = jnp.log1p(jnp.exp(-jnp.abs(z)))
            log_betas.append(-(jnp.maximum(-z, 0.0) + sp))
            log_1m = jnp.where(causal, -(jnp.maximum(z, 0.0) + sp), 0.0)
            hi = log_1m.astype(BF16)
            lo = (log_1m - hi.astype(F32)).astype(BF16)
            tails.append(_dot(jnp.concatenate([hi, lo], axis=1), after2))
            log_1ms.append(log_1m)
        alive = None
        for h in range(N_HEADS):
            c = c_ref[h]
            a = jnp.where(causal, jnp.exp(log_betas[h] + (tails[h] + c)), 0.0)
            acc_ref[h] += _dot(a.astype(BF16), vt)
            c_new = c + jnp.sum(log_1ms[h], axis=1, keepdims=True)
            c_ref[h] = c_new
            alive = c_new if alive is None else jnp.maximum(alive, c_new)
        return j - 1, jnp.max(alive)

    lax.while_loop(cond, body, (qi, jnp.float32(0.0)))
    out = acc_ref[0]
    for h in range(1, N_HEADS):
        out = jnp.where(head == h, acc_ref[h], out)
    o_ref[0] = out


def _stickbreak(qc, kc, vc):
    b, s, _ = qc.shape
    tq = DENSE_TQ
    seq = pl.BlockSpec((1, s, 256), lambda i, j: (i, 0, 0))
    tile = pl.BlockSpec((1, tq, 256), lambda i, j: (i, j, 0))
    return pl.pallas_call(
        _sb_kernel,
        grid=(b, s // tq),
        in_specs=[tile, seq, seq],
        out_specs=tile,
        out_shape=jax.ShapeDtypeStruct((b, s, 256), F32),
        scratch_shapes=[pltpu.VMEM((N_HEADS, tq, LANES), F32), pltpu.VMEM((N_HEADS, tq, 256), F32)],
        compiler_params=_params("parallel", "arbitrary"),
        name="stickbreak",
    )(qc, kc, vc)


def _dil_kernel(q_ref, k_ref, v_ref, o_ref, lse_ref, *, window, span):
    tq = q_ref.shape[1]
    q0 = pl.program_id(1) * tq
    k0 = pl.multiple_of(jnp.maximum(q0 + tq - span, 0), tq)
    q = q_ref[0]
    kt = k_ref[0, pl.ds(k0, span), :]
    vt = v_ref[0, pl.ds(k0, span), :]
    head = _head_mask((tq, 256))
    rows = lax.broadcasted_iota(jnp.int32, (tq, span), 0)
    cols = lax.broadcasted_iota(jnp.int32, (tq, span), 1)
    dist = (q0 + rows) - (k0 + cols)
    qh = [_keep_head(q, head, h) for h in range(N_HEADS)]
    o, m, l = _softmax_once(qh, [kt] * N_HEADS, [vt] * N_HEADS, (dist >= 0) & (dist <= window))
    out = o[0]
    lse = jnp.broadcast_to(m[0] + jnp.log(l[0]), (tq, 256))
    for h in range(1, N_HEADS):
        out = jnp.where(head == h, o[h], out)
        lse = jnp.where(head == h, m[h] + jnp.log(l[h]), lse)
    o_ref[0] = out
    lse_ref[0] = lse


def _dilated_group(q, k, v, window):
    n, L, _ = q.shape
    tq = DENSE_TQ
    assert window == tq
    kern = functools.partial(_dil_kernel, window=window, span=min(2 * tq, L))
    seq = pl.BlockSpec((1, L, 256), lambda i, j: (i, 0, 0))
    tile = pl.BlockSpec((1, tq, 256), lambda i, j: (i, j, 0))
    return pl.pallas_call(
        kern,
        grid=(n, L // tq),
        in_specs=[tile, seq, seq],
        out_specs=[tile, tile],
        out_shape=[jax.ShapeDtypeStruct((n, L, 256), F32)] * 2,
        compiler_params=_params("parallel", "arbitrary"),
        name="dilated",
    )(q, k, v)


_C_GATE, _C_MERGE, _C_NSAG = 0, 1024, 5120
_W2_COLS = 5888


def _out_kernel(x_ref, nw_ref, w2_ref, oa_ref, ocmp_ref, oslc_ref, owin_ref, oc_ref,
                od0_ref, od1_ref, od2_ref, ls0_ref, ls1_ref, ls2_ref,
                wup_ref, wout_ref, fnw_ref, y_ref, acc_ref, sub_ref, *, final):
    ts = x_ref.shape[1]
    x = x_ref[0]
    h = _rmsnorm(x, nw_ref[...]).astype(BF16)

    def proj(c0, w):
        return _dot(h, w2_ref[:, c0:c0 + w])

    def token_order(ref, dil, slot):
        if dil == 1:
            return ref[0, 0]
        n = ts // dil
        for c in range(dil):
            for half in range(2):
                sub_ref[slot, half, pl.ds(c, n, stride=dil), :] = ref[0, c, :, half * 128:(half + 1) * 128]
        return jnp.concatenate([sub_ref[slot, 0], sub_ref[slot, 1]], axis=1)

    ng = jax.nn.sigmoid(proj(_C_NSAG, 768))
    o_b = ng[:, 0:256] * ocmp_ref[0] + ng[:, 256:512] * oslc_ref[0] + ng[:, 512:768] * owin_ref[0]

    dils = [dil for _, dil in D_PATTERNS]
    od = [token_order(r, d, i) for i, (r, d) in enumerate(zip((od0_ref, od1_ref, od2_ref), dils))]
    ls = [token_order(r, d, 3 + i) for i, (r, d) in enumerate(zip((ls0_ref, ls1_ref, ls2_ref), dils))]
    mx = jnp.maximum(jnp.maximum(ls[0], ls[1]), ls[2])
    e = [jnp.exp(l - mx) for l in ls]
    den = e[0] + e[1] + e[2]
    o_d = (e[0] / den) * od[0] + (e[1] / den) * od[1] + (e[2] / den) * od[2]

    branches = (oa_ref[0], o_b, oc_ref[0], o_d)
    acc_ref[...] = jnp.zeros(acc_ref.shape, F32)
    for i in range(N_BRANCH):
        g = proj(_C_GATE + 256 * i, 256)
        wide = (branches[i] * (g * jax.nn.sigmoid(g))).astype(BF16)
        for half in range(2):
            c0 = 512 * half
            u = _dot(wide, wup_ref[i, :, c0:c0 + 512])
            mg = jax.nn.sigmoid(proj(_C_MERGE + 1024 * i + c0, 512))
            acc_ref[:, c0:c0 + 512] += mg * u
    out = x + _dot(acc_ref[...].astype(BF16), wout_ref[...])
    if final:
        out = _rmsnorm(out, fnw_ref[...])
    y_ref[0] = out


def _merge_out(x, norm_w, w2, token_outs, sub_outs, w_up, w_out, final_norm_w, final):
    b, s, d = x.shape
    ts = TOK_TILE
    tile256 = pl.BlockSpec((1, ts, 256), lambda i, j: (i, j, 0))
    sub_specs = [pl.BlockSpec((1, dil, ts // dil, 256), lambda i, j: (i, 0, j, 0)) for _, dil in D_PATTERNS]
    const = lambda a: pl.BlockSpec(a.shape, lambda i, j: (0,) * a.ndim)
    kern = functools.partial(_out_kernel, final=final)
    return pl.pallas_call(
        kern,
        grid=(b, s // ts),
        in_specs=[pl.BlockSpec((1, ts, d), lambda i, j: (i, j, 0)), const(norm_w), const(w2)]
        + [tile256] * 5 + sub_specs * 2 + [const(w_up), const(w_out), const(final_norm_w)],
        out_specs=pl.BlockSpec((1, ts, d), lambda i, j: (i, j, 0)),
        out_shape=jax.ShapeDtypeStruct((b, s, d), F32),
        scratch_shapes=[pltpu.VMEM((ts, d), F32), pltpu.VMEM((6, 2, ts, 128), F32)],
        compiler_params=_params("parallel", "arbitrary"),
        name="merge_out",
    )(x, norm_w, w2, *token_outs, *sub_outs, w_up, w_out, final_norm_w)


def _rope_tables(s):
    half = HEAD_DIM // 2
    inv = ROPE_THETA ** (-jnp.arange(half, dtype=F32) / half)
    ang = jnp.arange(s).astype(F32)[:, None] * inv[None, :]
    cos, sin = jnp.cos(ang), jnp.sin(ang)
    cos_t = jnp.tile(jnp.concatenate([cos, cos], axis=-1), (1, N_HEADS))
    sin_t = jnp.tile(jnp.concatenate([-sin, sin], axis=-1), (1, N_HEADS))
    return cos_t, sin_t


def _layer_weights(w_in, cmp_pos, cmp_w1, cmp_w2):
    o = np.concatenate([[0], np.cumsum(IN_SIZES)])
    qa, ka, va, ga, qb, kvb, gb, nsag, qkvc, gc, qkvd, gd, merge = [int(v) for v in o[:-1]]
    cols = lambda a, w: w_in[:, a:a + w]
    hd = HEAD_DIM
    w1 = jnp.concatenate([
        cols(qa, 256), cols(ka, 256), cols(va, 256), cols(qb, 256),
        cols(kvb + 2 * hd, hd), cols(kvb + 4 * hd, hd),
        cols(kvb, 2 * hd),
        cols(kvb + 3 * hd, hd), cols(kvb + 5 * hd, hd),
        cols(qkvc, 768), cols(qkvd, 2304)], axis=1).astype(BF16)
    gate_cols = jnp.concatenate([cols(ga, 256), cols(gb, 256), cols(gc, 256), cols(gd, 256)], axis=1)
    nsag_cols = jnp.repeat(cols(nsag, 3 * N_HEADS), HEAD_DIM, axis=1)
    w2 = jnp.concatenate([gate_cols, cols(merge, 4096), nsag_cols], axis=1).astype(BF16)

    half = B_CMP_STRIDE
    pe = jnp.concatenate([cmp_pos[0], cmp_pos[1]], axis=-1)
    pe_lo = pe[:half].reshape(1, half * 128)
    pe_hi = pe[half:].reshape(1, half * 128)
    w1k = cmp_w1[0].reshape(B_CMP_LEN, hd, B_CMP_HIDDEN)
    w1v = cmp_w1[1].reshape(B_CMP_LEN, hd, B_CMP_HIDDEN)
    zw = jnp.zeros_like(w1k)
    w1_big = jnp.concatenate([jnp.concatenate([w1k, zw], axis=2),
                              jnp.concatenate([zw, w1v], axis=2)], axis=1)
    w_lo = w1_big[:half].reshape(half * 128, 2 * B_CMP_HIDDEN).astype(BF16)
    w_hi = w1_big[half:].reshape(half * 128, 2 * B_CMP_HIDDEN).astype(BF16)

    def placed(w2h, top):
        blocks = []
        for h in range(N_HEADS):
            blk = jnp.zeros((2 * B_CMP_HIDDEN, 256), F32)
            r0 = 0 if top else B_CMP_HIDDEN
            blk = blk.at[r0:r0 + B_CMP_HIDDEN, h * hd:(h + 1) * hd].set(w2h)
            blocks.append(blk)
        return jnp.concatenate(blocks, axis=1).astype(BF16)

    return w1, w2, pe_lo, pe_hi, w_lo, w_hi, placed(cmp_w2[0], True), placed(cmp_w2[1], False)


def _overlap_t(s):
    nr = s // B_CMP_STRIDE
    nc = nr - 1
    nsel = s // B_SLC_LEN
    starts = np.arange(nr) * B_CMP_STRIDE
    j = np.arange(nsel)
    ov = ((starts[None, :] < (j[:, None] + 1) * B_SLC_LEN)
          & (starts[None, :] + B_CMP_LEN > j[:, None] * B_SLC_LEN)
          & (np.arange(nr)[None, :] < nc))
    return jnp.asarray(ov.astype(np.float32)).astype(BF16)


def _layer(x, norm_w, w_in, cmp_pos, cmp_w1, cmp_w2, w_up, w_out, final_norm_w, final, tables, ovt):
    b, s, _ = x.shape
    w1, w2, pe_lo, pe_hi, w_lo, w_hi, w2k, w2v = _layer_weights(w_in, cmp_pos, cmp_w1, cmp_w2)
    nw = norm_w.reshape(1, -1)
    (qa, ka, va, kmean, qb, qn, ksa, kwp, vsvw, kcvc, qc, kc, vc, *qkvd) = _inproj(x, nw, w1, *tables)

    o_a = _moba(qa, ka, va, kmean.transpose(0, 2, 1, 3))
    o_cmp, qbias = _nsa_cmp(kcvc, qb, pe_lo, pe_hi, w_lo, w_hi, w2k, w2v, ovt)
    o_slc, o_win = _nsa_slc_win(qn, qbias, ksa, kwp, vsvw)
    o_c = _stickbreak(qc, kc, vc)

    od, ls = [], []
    for g, (window, dil) in enumerate(D_PATTERNS):
        q, k, v = (a.reshape(b * dil, s // dil, 256) for a in qkvd[3 * g:3 * g + 3])
        o, l = _dilated_group(q, k, v, window // dil)
        od.append(o.reshape(b, dil, s // dil, 256))
        ls.append(l.reshape(b, dil, s // dil, 256))

    return _merge_out(x, nw, w2, (o_a, o_cmp, o_slc, o_win, o_c), (*od, *ls),
                      w_up.astype(BF16), w_out.astype(BF16), final_norm_w.reshape(1, -1), final)


def kernel(x, norm_w, w_in, nsa_cmp_pos, nsa_cmp_w1, nsa_cmp_w2, w_up, w_out, final_norm_w):
    depth = norm_w.shape[0]
    s = x.shape[1]
    tables = _rope_tables(s)
    ovt = _overlap_t(s)
    for layer in range(depth):
        x = _layer(x, norm_w[layer], w_in[layer], nsa_cmp_pos[layer], nsa_cmp_w1[layer],
                   nsa_cmp_w2[layer], w_up[layer], w_out[layer], final_norm_w,
                   layer == depth - 1, tables, ovt)
    return x
```

```python
import functools

import numpy as np
import jax
import jax.numpy as jnp
from jax import lax
from jax.experimental import pallas as pl
from jax.experimental.pallas import tpu as pltpu

F32 = jnp.float32
BF16 = jnp.bfloat16

HEAD_DIM = 64
N_HEADS = 4
BRANCH_W = N_HEADS * HEAD_DIM
ROPE_THETA = 10000.0
NORM_EPS = 1e-6
QK_SCALE = HEAD_DIM ** -0.5

A_BLOCK = 256
A_SHIFT = 8
A_TOPK = 3
B_CMP_LEN = 32
B_CMP_STRIDE = 16
B_CMP_HIDDEN = 256
B_SLC_LEN = 64
B_SLC_SHIFT = 6
B_SLC_TOPN = 16
B_WINDOW = 512
D_PATTERNS = ((128, 1), (512, 4), (2048, 16))
N_BRANCH = 4

IN_SIZES = (
    BRANCH_W, BRANCH_W, BRANCH_W, BRANCH_W,
    BRANCH_W, 6 * HEAD_DIM, BRANCH_W, 3 * N_HEADS,
    3 * BRANCH_W, BRANCH_W,
    3 * len(D_PATTERNS) * N_HEADS * HEAD_DIM, BRANCH_W,
    N_BRANCH * 1024,
)

M_INIT = -1e30
MASKED = -2e30
SB_DEAD = -120.0
VMEM_LIMIT = 56 * 1024 * 1024
LANES = 128

TOK_TILE = 256
DENSE_TQ = 128


def _dot(a, b):
    return jnp.dot(a, b, preferred_element_type=F32)


def _dot_nt(a, b):
    return lax.dot_general(a, b, (((1,), (1,)), ((), ())), preferred_element_type=F32)


def _params(*sem):
    return pltpu.CompilerParams(dimension_semantics=sem, vmem_limit_bytes=VMEM_LIMIT)


def _rmsnorm(x, w):
    y = x * lax.rsqrt(jnp.mean(x * x, axis=-1, keepdims=True) + NORM_EPS)
    return y * w


def _head_mask(shape):
    return lax.broadcasted_iota(jnp.int32, shape, 1) >> 6


def _keep_head(q, head, h):
    return jnp.where(head == h, q.astype(F32), 0.0).astype(BF16)


def _with_mask_rows(q_t, bias_t):
    n, tq = bias_t.shape
    parts = [q_t[:HEAD_DIM], bias_t.astype(BF16)]
    if HEAD_DIM + n < LANES:
        parts.append(jnp.zeros((LANES - HEAD_DIM - n, tq), BF16))
    return jnp.concatenate(parts, axis=0)


def _heads_to_tokens(outs_t):
    return jnp.transpose(jnp.concatenate(outs_t, axis=0))


def _flash_init(m_ref, l_ref, acc_ref):
    m_ref[...] = jnp.full(m_ref.shape, M_INIT, F32)
    l_ref[...] = jnp.zeros(l_ref.shape, F32)
    acc_ref[...] = jnp.zeros(acc_ref.shape, F32)


def _flash_step_t(ks, q_ts, v_ts, keep, m_ref, l_ref, acc_ref):
    scores = [_dot(k, q_t) for k, q_t in zip(ks, q_ts)]
    for h, (s, v_t) in enumerate(zip(scores, v_ts)):
        if keep is not None:
            s = jnp.where(keep, s, MASKED)
        m_old = m_ref[h]
        m_new = jnp.maximum(m_old, jnp.max(s, axis=0, keepdims=True))
        p = jnp.exp(s - m_new)
        alpha = jnp.exp(m_old - m_new)
        l_ref[h] = alpha * l_ref[h] + jnp.sum(p, axis=0, keepdims=True)
        acc_ref[h] = alpha * acc_ref[h] + _dot(v_t, p.astype(BF16))
        m_ref[h] = m_new


def _softmax_once(qs, ks, vs, keeps):
    scores = [jnp.where(keep, _dot_nt(q, k), MASKED) for q, k, keep in zip(qs, ks, keeps)]
    outs, maxes, sums = [], [], []
    for s, v in zip(scores, vs):
        m = jnp.max(s, axis=1, keepdims=True)
        p = jnp.exp(s - m)
        l = jnp.sum(p, axis=1, keepdims=True)
        outs.append(_dot(p.astype(BF16), v) / l)
        maxes.append(m)
        sums.append(l)
    return outs, maxes, sums


_C_QA, _C_KA, _C_VA, _C_QB = 0, 256, 512, 768
_C_KSKW, _C_KCVC, _C_VSVW = 1024, 1152, 1280
_C_QC, _C_KC, _C_VC = 1408, 1664, 1920
_C_QD, _C_KD, _C_VD = 2176, 2944, 3712
_W1_COLS = 4480


def _inproj_kernel(x_ref, nw_ref, w_ref, cos_ref, sin_ref,
                   qat_ref, ka_ref, vat_ref, kmean_ref,
                   qb_ref, qnt_ref, ksa_ref, kwp_ref, vst_ref, vwt_ref, kcvc_ref,
                   qc_ref, kc_ref, vc_ref,
                   qd0_ref, kd0_ref, vd0_ref, qd1_ref, kd1_ref, vd1_ref, qd2_ref, kd2_ref, vd2_ref,
                   sub_ref):
    ts = x_ref.shape[1]
    pos0 = pl.program_id(1) * ts
    h = _rmsnorm(x_ref[0], nw_ref[...]).astype(BF16)
    cos = cos_ref[...]
    sin = sin_ref[...]
    lane = lax.broadcasted_iota(jnp.int32, (ts, 256), 1)
    first = (lane & 63) < 32
    lane128 = lax.broadcasted_iota(jnp.int32, (ts, 128), 1)
    pos128 = pos0 + lax.broadcasted_iota(jnp.int32, (ts, 128), 0)
    first128 = (lane128 & 63) < 32
    lo128 = lane128 < 64

    def proj(c0, w=256):
        return _dot(h, w_ref[:, c0:c0 + w])

    def rope(v):
        partner = jnp.where(first, pltpu.roll(v, 224, 1), pltpu.roll(v, 32, 1))
        return v * cos + partner * sin

    def split_heads(v):
        out = []
        for pair in range(2):
            p = v[:, pair * 128:(pair + 1) * 128]
            out.append(jnp.where(lo128, p, 0.0))
            out.append(jnp.where(lo128, pltpu.roll(p, 64, 1), 0.0))
        return out

    def store_q_t(ref, v):
        v_t = jnp.transpose(v)
        for hd in range(N_HEADS):
            ref[0, hd, 0:HEAD_DIM, :] = v_t[hd * HEAD_DIM:(hd + 1) * HEAD_DIM].astype(BF16)
            ref[0, hd, HEAD_DIM:LANES, :] = jnp.zeros((LANES - HEAD_DIM, ts), BF16)

    store_q_t(qat_ref, rope(proj(_C_QA)) * QK_SCALE)
    ka = split_heads(rope(proj(_C_KA)))
    va_t = jnp.transpose(proj(_C_VA))
    block_hot = jnp.where(lane128 - 64 == (pos128 >> A_SHIFT), 1.0, 0.0)
    for hd in range(N_HEADS):
        ka_ref[0, hd] = (ka[hd] + block_hot).astype(BF16)
        vat_ref[0, hd, 0] = va_t[hd * HEAD_DIM:(hd + 1) * HEAD_DIM].astype(BF16)
        kmean_ref[0, 0, pl.ds(hd, 1), :] = jnp.sum(ka[hd], axis=0, keepdims=True) * (1.0 / ts)

    qb = proj(_C_QB)
    qb_ref[0] = (qb * QK_SCALE).astype(BF16)
    store_q_t(qnt_ref, rope(qb) * QK_SCALE)
    kskw = proj(_C_KSKW, 128)
    partner = jnp.where(first128, pltpu.roll(kskw, 96, 1), pltpu.roll(kskw, 32, 1))
    kskw = kskw * cos[:, :128] + partner * sin[:, :128]
    slc_hot = jnp.where(lane128 - 64 == (pos128 >> B_SLC_SHIFT), 1.0, 0.0)
    ksa_ref[0] = jnp.where(lo128, kskw, slc_hot).astype(BF16)
    kwp_ref[0] = jnp.where(lo128, pltpu.roll(kskw, 64, 1), 0.0).astype(BF16)
    vsvw_t = jnp.transpose(proj(_C_VSVW, 128))
    vst_ref[0, 0] = vsvw_t[0:HEAD_DIM].astype(BF16)
    vwt_ref[0, 0] = vsvw_t[HEAD_DIM:2 * HEAD_DIM].astype(BF16)
    sub_ref[0] = proj(_C_KCVC, 128)
    for t in range(B_CMP_STRIDE):
        kcvc_ref[0, :, t * 128:(t + 1) * 128] = sub_ref[0, pl.ds(t, ts // B_CMP_STRIDE, stride=B_CMP_STRIDE), :]

    qc_ref[0] = (proj(_C_QC) * QK_SCALE).astype(BF16)
    kc_ref[0] = proj(_C_KC).astype(BF16)
    vc_ref[0] = proj(_C_VC).astype(BF16)

    def store_sub(ref, v, dil):
        if dil == 1:
            ref[0, 0] = v.astype(BF16)
            return
        n = ts // dil
        for half in range(2):
            sub_ref[half] = v[:, half * 128:(half + 1) * 128]
        for c in range(dil):
            for half in range(2):
                ref[0, c, :, half * 128:(half + 1) * 128] = (
                    sub_ref[half, pl.ds(c, n, stride=dil), :].astype(BF16))

    groups = ((qd0_ref, kd0_ref, vd0_ref), (qd1_ref, kd1_ref, vd1_ref), (qd2_ref, kd2_ref, vd2_ref))
    for g, (q_ref, k_ref, v_ref) in enumerate(groups):
        dil = D_PATTERNS[g][1]
        store_sub(q_ref, rope(proj(_C_QD + 256 * g)) * QK_SCALE, dil)
        store_sub(k_ref, rope(proj(_C_KD + 256 * g)), dil)
        store_sub(v_ref, proj(_C_VD + 256 * g), dil)


def _inproj(x, norm_w, w1, cos_t, sin_t):
    b, s, d = x.shape
    ts = TOK_TILE
    nt = s // ts
    tok = lambda w, dt: jax.ShapeDtypeStruct((b, s, w), dt)
    tok_spec = lambda w: pl.BlockSpec((1, ts, w), lambda i, j: (i, j, 0))
    head = jax.ShapeDtypeStruct((b, N_HEADS, s, 128), BF16)
    head_spec = pl.BlockSpec((1, N_HEADS, ts, 128), lambda i, j: (i, 0, j, 0))
    q_t = jax.ShapeDtypeStruct((b, N_HEADS, LANES, s), BF16)
    q_t_spec = pl.BlockSpec((1, N_HEADS, LANES, ts), lambda i, j: (i, 0, 0, j))
    v_t = jax.ShapeDtypeStruct((b, nt, HEAD_DIM, ts), BF16)
    v_t_spec = pl.BlockSpec((1, 1, HEAD_DIM, ts), lambda i, j: (i, j, 0, 0))
    nrow = ts // B_CMP_STRIDE
    out_shape = [q_t, head, jax.ShapeDtypeStruct((b, N_HEADS, nt, HEAD_DIM, ts), BF16),
                 jax.ShapeDtypeStruct((b, nt, N_HEADS, 128), F32),
                 tok(256, BF16), q_t, tok(128, BF16), tok(128, BF16), v_t, v_t,
                 jax.ShapeDtypeStruct((b, s // B_CMP_STRIDE, B_CMP_STRIDE * 128), F32),
                 tok(256, BF16), tok(256, BF16), tok(256, BF16)]
    out_specs = [q_t_spec, head_spec,
                 pl.BlockSpec((1, N_HEADS, 1, HEAD_DIM, ts), lambda i, j: (i, 0, j, 0, 0)),
                 pl.BlockSpec((1, 1, N_HEADS, 128), lambda i, j: (i, j, 0, 0)),
                 tok_spec(256), q_t_spec, tok_spec(128), tok_spec(128), v_t_spec, v_t_spec,
                 pl.BlockSpec((1, nrow, B_CMP_STRIDE * 128), lambda i, j: (i, j, 0)),
                 tok_spec(256), tok_spec(256), tok_spec(256)]
    for _, dil in D_PATTERNS:
        out_shape += [jax.ShapeDtypeStruct((b, dil, s // dil, 256), BF16)] * 3
        out_specs += [pl.BlockSpec((1, dil, ts // dil, 256), lambda i, j: (i, 0, j, 0))] * 3
    return pl.pallas_call(
        _inproj_kernel,
        grid=(b, nt),
        in_specs=[
            pl.BlockSpec((1, ts, d), lambda i, j: (i, j, 0)),
            pl.BlockSpec((1, d), lambda i, j: (0, 0)),
            pl.BlockSpec((d, _W1_COLS), lambda i, j: (0, 0)),
            pl.BlockSpec((ts, 256), lambda i, j: (j, 0)),
            pl.BlockSpec((ts, 256), lambda i, j: (j, 0)),
        ],
        out_specs=out_specs,
        out_shape=out_shape,
        scratch_shapes=[pltpu.VMEM((2, ts, 128), F32)],
        compiler_params=_params("parallel", "arbitrary"),
        name="inproj",
    )(x, norm_w, w1, cos_t, sin_t)


def _moba_kernel(qt_ref, k_ref, vt_ref, km_ref, o_ref, qc_ref, m_ref, l_ref, acc_ref, *, nb, topk):
    tq = qt_ref.shape[3]
    tk = tq
    qb = pl.program_id(1)
    blk_t = lax.broadcasted_iota(jnp.int32, (nb, tq), 0)
    past_t = blk_t < qb
    key = lax.broadcasted_iota(jnp.int32, (tk, tq), 0)
    qry = lax.broadcasted_iota(jnp.int32, (tk, tq), 1)
    causal = key <= qry

    for h in range(N_HEADS):
        q_t = qt_ref[0, h]
        gate = jnp.where(past_t, _dot(km_ref[0, h].astype(BF16), q_t), -jnp.inf)
        rank = jnp.zeros((nb, tq), jnp.int32)
        for i in range(nb):
            row = gate[i:i + 1, :]
            beats = (row > gate) | ((row == gate) & (blk_t > i))
            rank = rank + jnp.where(beats, 1, 0)
        keep = ((rank < topk) & past_t) | (blk_t == qb)
        qc_ref[h] = _with_mask_rows(q_t, jnp.where(keep, 0.0, MASKED))

    _flash_init(m_ref, l_ref, acc_ref)

    def step(j, keep):
        off = pl.multiple_of(j * tk, tk)
        _flash_step_t([k_ref[0, h, pl.ds(off, tk), :] for h in range(N_HEADS)],
                      [qc_ref[h] for h in range(N_HEADS)],
                      [vt_ref[0, h, j] for h in range(N_HEADS)],
                      keep, m_ref, l_ref, acc_ref)

    def body(j, carry):
        step(j, None)
        return carry

    lax.fori_loop(0, qb, body, 0)
    step(qb, causal)
    o_ref[0] = _heads_to_tokens([acc_ref[h] / l_ref[h] for h in range(N_HEADS)])


def _moba(qat, ka, vat, kmean):
    b, _, s, _ = ka.shape
    tq = A_BLOCK
    nb = s // A_BLOCK
    kern = functools.partial(_moba_kernel, nb=nb, topk=min(A_TOPK, nb))
    row = pltpu.VMEM((N_HEADS, 1, tq), F32)
    return pl.pallas_call(
        kern,
        grid=(b, nb),
        in_specs=[pl.BlockSpec((1, N_HEADS, LANES, tq), lambda i, j: (i, 0, 0, j)),
                  pl.BlockSpec((1, N_HEADS, s, 128), lambda i, j: (i, 0, 0, 0)),
                  pl.BlockSpec((1, N_HEADS, nb, HEAD_DIM, tq), lambda i, j: (i, 0, 0, 0, 0)),
                  pl.BlockSpec((1, N_HEADS, nb, 128), lambda i, j: (i, 0, 0, 0))],
        out_specs=pl.BlockSpec((1, tq, 256), lambda i, j: (i, j, 0)),
        out_shape=jax.ShapeDtypeStruct((b, s, 256), F32),
        scratch_shapes=[pltpu.VMEM((N_HEADS, LANES, tq), BF16), row, row,
                        pltpu.VMEM((N_HEADS, HEAD_DIM, tq), F32)],
        compiler_params=_params("parallel", "arbitrary"),
        name="moba",
    )(qat, ka, vat, kmean)


def _gelu_tanh(x):
    return 0.5 * x * (1.0 + jnp.tanh(np.sqrt(2.0 / np.pi).astype(np.float32) * (x + 0.044715 * (x * x * x))))


def _cmp_kernel(xr_ref, pelo_ref, pehi_ref, wlo_ref, whi_ref, w2k_ref, w2v_ref, q_ref, ovt_ref,
                o_ref, qbias_ref, kbig_ref, vbig_ref, *, nsel, topn):
    tq = q_ref.shape[1]
    nr = xr_ref.shape[1]
    nc = nr - 1
    qi = pl.program_id(1)

    @pl.when(qi == 0)
    def _():
        xr = xr_ref[0]
        lo = (xr + pelo_ref[...]).astype(BF16)
        hi = (pltpu.roll(xr, nr - 1, 0) + pehi_ref[...]).astype(BF16)
        hid = _dot(lo, wlo_ref[...]) + _dot(hi, whi_ref[...])
        act = _gelu_tanh(hid).astype(BF16)
        kbig_ref[...] = _dot(act, w2k_ref[...]).astype(BF16)
        vbig_ref[...] = _dot(act, w2v_ref[...]).astype(BF16)

    q = q_ref[0]
    head = _head_mask((tq, 256))
    tpos = qi * tq + lax.broadcasted_iota(jnp.int32, (tq, nr), 0)
    n = lax.broadcasted_iota(jnp.int32, (tq, nr), 1)
    vis = (n * B_CMP_STRIDE + (B_CMP_LEN - 1) <= tpos) & (n < nc)
    ovt = ovt_ref[...]
    out = jnp.zeros((tq, 256), F32)
    imp = jnp.zeros((nsel, tq), F32)
    for h in range(N_HEADS):
        qh = _keep_head(q, head, h)
        s = jnp.where(vis, _dot_nt(qh, kbig_ref[:, h * 256:(h + 1) * 256]), MASKED)
        mx = jnp.max(s, axis=1, keepdims=True)
        e = jnp.where(vis, jnp.exp(s - mx), 0.0)
        den = jnp.maximum(jnp.sum(e, axis=1, keepdims=True), 1e-30)
        p = (e / den).astype(BF16)
        out = out + _dot(p, vbig_ref[:, h * 256:(h + 1) * 256])
        imp = imp + _dot_nt(ovt, p)
    o_ref[0] = out

    jj = lax.broadcasted_iota(jnp.int32, (nsel, tq), 0)
    cur = (qi * tq + lax.broadcasted_iota(jnp.int32, (nsel, tq), 1)) >> B_SLC_SHIFT
    forced = (jj == 0) | (jj == cur) | (jj == cur - 1)
    valid = jj <= cur
    imp = jnp.where(valid, jnp.where(forced, jnp.inf, imp), -jnp.inf)
    rank = jnp.zeros((nsel, tq), jnp.int32)
    for i in range(nsel):
        row = imp[i:i + 1, :]
        beats = (row > imp) | ((row == imp) & (jj > i))
        rank = rank + jnp.where(beats, 1, 0)
    qbias_ref[0] = jnp.where((rank < topn) & valid, 0.0, MASKED).astype(BF16)


def _nsa_cmp(xr, qb, pe_lo, pe_hi, w_lo, w_hi, w2k, w2v, ovt):
    b, s, _ = qb.shape
    nr = s // B_CMP_STRIDE
    nsel = s // B_SLC_LEN
    assert nsel <= LANES - HEAD_DIM, "the selection mask lives in the spare contraction rows of q"
    tq = TOK_TILE
    kern = functools.partial(_cmp_kernel, nsel=nsel, topn=min(B_SLC_TOPN, nsel))
    full = lambda a: pl.BlockSpec(a.shape, lambda i, j: (0,) * a.ndim)
    return pl.pallas_call(
        kern,
        grid=(b, s // tq),
        in_specs=[pl.BlockSpec((1, nr, B_CMP_STRIDE * 128), lambda i, j: (i, 0, 0)),
                  full(pe_lo), full(pe_hi), full(w_lo), full(w_hi), full(w2k), full(w2v),
                  pl.BlockSpec((1, tq, 256), lambda i, j: (i, j, 0)), full(ovt)],
        out_specs=[pl.BlockSpec((1, tq, 256), lambda i, j: (i, j, 0)),
                   pl.BlockSpec((1, nsel, tq), lambda i, j: (i, 0, j))],
        out_shape=[jax.ShapeDtypeStruct((b, s, 256), F32),
                   jax.ShapeDtypeStruct((b, nsel, s), BF16)],
        scratch_shapes=[pltpu.VMEM((nr, 1024), BF16), pltpu.VMEM((nr, 1024), BF16)],
        compiler_params=_params("parallel", "arbitrary"),
        name="nsa_cmp",
    )(xr, pe_lo, pe_hi, w_lo, w_hi, w2k, w2v, qb, ovt)


def _nsa_kernel(qt_ref, qbias_ref, ksa_ref, kwp_ref, vst_ref, vwt_ref, oslc_ref, owin_ref,
                m_ref, l_ref, acc_ref, *, ntile):
    tq = qt_ref.shape[3]
    tk = tq
    qi = pl.program_id(1)
    q_t = [qt_ref[0, h] for h in range(N_HEADS)]
    key = lax.broadcasted_iota(jnp.int32, (tk, tq), 0)
    qry = lax.broadcasted_iota(jnp.int32, (tk, tq), 1)

    bias_t = qbias_ref[0]
    qs_t = [_with_mask_rows(q_t[h], bias_t) for h in range(N_HEADS)]
    _flash_init(m_ref, l_ref, acc_ref)

    def step(j, keep):
        kt = ksa_ref[0, pl.ds(pl.multiple_of(j * tk, tk), tk), :]
        _flash_step_t([kt] * N_HEADS, qs_t, [vst_ref[0, j]] * N_HEADS, keep, m_ref, l_ref, acc_ref)

    def slc_body(j, carry):
        step(j, None)
        return carry

    lax.fori_loop(0, qi, slc_body, 0)
    step(qi, key <= qry)
    oslc_ref[0] = _heads_to_tokens([acc_ref[h] / jnp.maximum(l_ref[h], 1e-30) for h in range(N_HEADS)])

    t0 = jnp.maximum(qi + 1 - ntile, 0)
    kt = kwp_ref[0, pl.ds(pl.multiple_of(t0 * tk, tk), ntile * tk), :]
    span = ntile * tk
    dist = (qi * tq + lax.broadcasted_iota(jnp.int32, (span, tq), 1)) - (
        t0 * tk + lax.broadcasted_iota(jnp.int32, (span, tq), 0))
    keep = (dist >= 0) & (dist < B_WINDOW)
    scores = [jnp.where(keep, _dot(kt, q_t[h]), MASKED) for h in range(N_HEADS)]
    outs = []
    for s in scores:
        m = jnp.max(s, axis=0, keepdims=True)
        p = jnp.exp(s - m)
        l = jnp.sum(p, axis=0, keepdims=True)
        p = p.astype(BF16)
        o = _dot(vwt_ref[0, t0], p[0:tk])
        for i in range(1, ntile):
            o = o + _dot(vwt_ref[0, t0 + i], p[i * tk:(i + 1) * tk])
        outs.append(o / l)
    owin_ref[0] = _heads_to_tokens(outs)


def _nsa_slc_win(qnt, qbias, ksa, kwp, vst, vwt):
    b, s, _ = ksa.shape
    tq = TOK_TILE
    nt = s // tq
    nsel = s // B_SLC_LEN
    ntile = min(B_WINDOW // tq + 1, nt)
    seq = pl.BlockSpec((1, s, 128), lambda i, j: (i, 0, 0))
    v_t = pl.BlockSpec((1, nt, HEAD_DIM, tq), lambda i, j: (i, 0, 0, 0))
    ospec = pl.BlockSpec((1, tq, 256), lambda i, j: (i, j, 0))
    row = pltpu.VMEM((N_HEADS, 1, tq), F32)
    return pl.pallas_call(
        functools.partial(_nsa_kernel, ntile=ntile),
        grid=(b, nt),
        in_specs=[pl.BlockSpec((1, N_HEADS, LANES, tq), lambda i, j: (i, 0, 0, j)),
                  pl.BlockSpec((1, nsel, tq), lambda i, j: (i, 0, j)), seq, seq, v_t, v_t],
        out_specs=[ospec, ospec],
        out_shape=[jax.ShapeDtypeStruct((b, s, 256), F32)] * 2,
        scratch_shapes=[row, row, pltpu.VMEM((N_HEADS, HEAD_DIM, tq), F32)],
        compiler_params=_params("parallel", "arbitrary"),
        name="nsa_slc_win",
    )(qnt, qbias, ksa, kwp, vst, vwt)


def _sb_kernel(q_ref, k_ref, v_ref, o_ref, c_ref, acc_ref):
    tq = q_ref.shape[1]
    tk = tq
    qi = pl.program_id(1)
    q = q_ref[0]
    head = _head_mask((tq, 256))
    qh = [_keep_head(q, head, h) for h in range(N_HEADS)]
    rows = lax.broadcasted_iota(jnp.int32, (tq, tk), 0)
    cols = lax.broadcasted_iota(jnp.int32, (tq, tk), 1)
    after = jnp.where(rows > cols, 1.0, 0.0).astype(BF16)
    after2 = jnp.concatenate([after, after], axis=0)
    c_ref[...] = jnp.zeros(c_ref.shape, F32)
    acc_ref[...] = jnp.zeros(acc_ref.shape, F32)

    def cond(carry):
        j, cmax = carry
        return (j >= 0) & (cmax > SB_DEAD)

    def body(carry):
        j, _ = carry
        off = pl.multiple_of(j * tk, tk)
        kt = k_ref[0, pl.ds(off, tk), :]
        vt = v_ref[0, pl.ds(off, tk), :]
        causal = (off + cols) < (qi * tq + rows)
        zs = [_dot_nt(qh[h], kt) for h in range(N_HEADS)]
        log_betas, log_1ms, tails = [], [], []
        for z in zs:
            sp = jnp.log1p(jnp.exp(-jnp.abs(z)))
            log_betas.append(-(jnp.maximum(-z, 0.0) + sp))
            log_1m = jnp.where(causal, -(jnp.maximum(z, 0.0) + sp), 0.0)
            hi = log_1m.astype(BF16)
            lo = (log_1m - hi.astype(F32)).astype(BF16)
            tails.append(_dot(jnp.concatenate([hi, lo], axis=1), after2))
            log_1ms.append(log_1m)
        alive = None
        for h in range(N_HEADS):
            c = c_ref[h]
            a = jnp.where(causal, jnp.exp(log_betas[h] + (tails[h] + c)), 0.0)
            acc_ref[h] += _dot(a.astype(BF16), vt)
            c_new = c + jnp.sum(log_1ms[h], axis=1, keepdims=True)
            c_ref[h] = c_new
            alive = c_new if alive is None else jnp.maximum(alive, c_new)
        return j - 1, jnp.max(alive)

    lax.while_loop(cond, body, (qi, jnp.float32(0.0)))
    out = acc_ref[0]
    for h in range(1, N_HEADS):
        out = jnp.where(head == h, acc_ref[h], out)
    o_ref[0] = out


def _stickbreak(qc, kc, vc):
    b, s, _ = qc.shape
    tq = DENSE_TQ
    seq = pl.BlockSpec((1, s, 256), lambda i, j: (i, 0, 0))
    tile = pl.BlockSpec((1, tq, 256), lambda i, j: (i, j, 0))
    return pl.pallas_call(
        _sb_kernel,
        grid=(b, s // tq),
        in_specs=[tile, seq, seq],
        out_specs=tile,
        out_shape=jax.ShapeDtypeStruct((b, s, 256), F32),
        scratch_shapes=[pltpu.VMEM((N_HEADS, tq, LANES), F32), pltpu.VMEM((N_HEADS, tq, 256), F32)],
        compiler_params=_params("parallel", "arbitrary"),
        name="stickbreak",
    )(qc, kc, vc)


def _dil_kernel(q_ref, k_ref, v_ref, o_ref, lse_ref, *, window, span, tq):
    nq = q_ref.shape[1] // tq
    head = _head_mask((tq, 256))
    rows = lax.broadcasted_iota(jnp.int32, (tq, span), 0)
    cols = lax.broadcasted_iota(jnp.int32, (tq, span), 1)
    qs, ks, vs, keeps = [], [], [], []
    for t in range(nq):
        q0 = (pl.program_id(1) * nq + t) * tq
        k0 = pl.multiple_of(jnp.maximum(q0 + tq - span, 0), tq)
        q = q_ref[0, t * tq:(t + 1) * tq, :]
        dist = (q0 + rows) - (k0 + cols)
        qs += [_keep_head(q, head, h) for h in range(N_HEADS)]
        ks += [k_ref[0, pl.ds(k0, span), :]] * N_HEADS
        vs += [v_ref[0, pl.ds(k0, span), :]] * N_HEADS
        keeps += [(dist >= 0) & (dist <= window)] * N_HEADS
    o, m, l = _softmax_once(qs, ks, vs, keeps)
    for t in range(nq):
        out = o[t * N_HEADS]
        lse = jnp.broadcast_to(m[t * N_HEADS] + jnp.log(l[t * N_HEADS]), (tq, 256))
        for h in range(1, N_HEADS):
            i = t * N_HEADS + h
            out = jnp.where(head == h, o[i], out)
            lse = jnp.where(head == h, m[i] + jnp.log(l[i]), lse)
        o_ref[0, t * tq:(t + 1) * tq, :] = out
        lse_ref[0, t * tq:(t + 1) * tq, :] = lse


def _dilated_group(q, k, v, window):
    n, L, _ = q.shape
    tq = DENSE_TQ
    assert window == tq
    nq = 2 if L % (2 * tq) == 0 else 1
    kern = functools.partial(_dil_kernel, window=window, span=min(2 * tq, L), tq=tq)
    seq = pl.BlockSpec((1, L, 256), lambda i, j: (i, 0, 0))
    tile = pl.BlockSpec((1, nq * tq, 256), lambda i, j: (i, j, 0))
    return pl.pallas_call(
        kern,
        grid=(n, L // (nq * tq)),
        in_specs=[tile, seq, seq],
        out_specs=[tile, tile],
        out_shape=[jax.ShapeDtypeStruct((n, L, 256), F32)] * 2,
        compiler_params=_params("parallel", "arbitrary"),
        name="dilated",
    )(q, k, v)


_C_GATE, _C_MERGE, _C_NSAG = 0, 1024, 5120
_W2_COLS = 5888


def _out_kernel(x_ref, nw_ref, w2_ref, oa_ref, ocmp_ref, oslc_ref, owin_ref, oc_ref,
                od0_ref, od1_ref, od2_ref, ls0_ref, ls1_ref, ls2_ref,
                wup_ref, wout_ref, fnw_ref, y_ref, acc_ref, sub_ref, *, final):
    ts = x_ref.shape[1]
    x = x_ref[0]
    h = _rmsnorm(x, nw_ref[...]).astype(BF16)

    def proj(c0, w):
        return _dot(h, w2_ref[:, c0:c0 + w])

    def token_order(ref, dil, slot):
        if dil == 1:
            return ref[0, 0]
        n = ts // dil
        for c in range(dil):
            for half in range(2):
                sub_ref[slot, half, pl.ds(c, n, stride=dil), :] = ref[0, c, :, half * 128:(half + 1) * 128]
        return jnp.concatenate([sub_ref[slot, 0], sub_ref[slot, 1]], axis=1)

    ng = jax.nn.sigmoid(proj(_C_NSAG, 768))
    o_b = ng[:, 0:256] * ocmp_ref[0] + ng[:, 256:512] * oslc_ref[0] + ng[:, 512:768] * owin_ref[0]

    dils = [dil for _, dil in D_PATTERNS]
    od = [token_order(r, d, i) for i, (r, d) in enumerate(zip((od0_ref, od1_ref, od2_ref), dils))]
    ls = [token_order(r, d, 3 + i) for i, (r, d) in enumerate(zip((ls0_ref, ls1_ref, ls2_ref), dils))]
    mx = jnp.maximum(jnp.maximum(ls[0], ls[1]), ls[2])
    e = [jnp.exp(l - mx) for l in ls]
    den = e[0] + e[1] + e[2]
    o_d = (e[0] / den) * od[0] + (e[1] / den) * od[1] + (e[2] / den) * od[2]

    branches = (oa_ref[0], o_b, oc_ref[0], o_d)
    acc_ref[...] = jnp.zeros(acc_ref.shape, F32)
    for i in range(N_BRANCH):
        g = proj(_C_GATE + 256 * i, 256)
        wide = (branches[i] * (g * jax.nn.sigmoid(g))).astype(BF16)
        for half in range(2):
            c0 = 512 * half
            u = _dot(wide, wup_ref[i, :, c0:c0 + 512])
            mg = jax.nn.sigmoid(proj(_C_MERGE + 1024 * i + c0, 512))
            acc_ref[:, c0:c0 + 512] += mg * u
    out = x + _dot(acc_ref[...].astype(BF16), wout_ref[...])
    if final:
        out = _rmsnorm(out, fnw_ref[...])
    y_ref[0] = out


def _merge_out(x, norm_w, w2, token_outs, sub_outs, w_up, w_out, final_norm_w, final):
    b, s, d = x.shape
    ts = TOK_TILE
    tile256 = pl.BlockSpec((1, ts, 256), lambda i, j: (i, j, 0))
    sub_specs = [pl.BlockSpec((1, dil, ts // dil, 256), lambda i, j: (i, 0, j, 0)) for _, dil in D_PATTERNS]
    const = lambda a: pl.BlockSpec(a.shape, lambda i, j: (0,) * a.ndim)
    kern = functools.partial(_out_kernel, final=final)
    return pl.pallas_call(
        kern,
        grid=(b, s // ts),
        in_specs=[pl.BlockSpec((1, ts, d), lambda i, j: (i, j, 0)), const(norm_w), const(w2)]
        + [tile256] * 5 + sub_specs * 2 + [const(w_up), const(w_out), const(final_norm_w)],
        out_specs=pl.BlockSpec((1, ts, d), lambda i, j: (i, j, 0)),
        out_shape=jax.ShapeDtypeStruct((b, s, d), F32),
        scratch_shapes=[pltpu.VMEM((ts, d), F32), pltpu.VMEM((6, 2, ts, 128), F32)],
        compiler_params=_params("parallel", "arbitrary"),
        name="merge_out",
    )(x, norm_w, w2, *token_outs, *sub_outs, w_up, w_out, final_norm_w)


def _rope_tables(s):
    half = HEAD_DIM // 2
    inv = ROPE_THETA ** (-jnp.arange(half, dtype=F32) / half)
    ang = jnp.arange(s).astype(F32)[:, None] * inv[None, :]
    cos, sin = jnp.cos(ang), jnp.sin(ang)
    cos_t = jnp.tile(jnp.concatenate([cos, cos], axis=-1), (1, N_HEADS))
    sin_t = jnp.tile(jnp.concatenate([-sin, sin], axis=-1), (1, N_HEADS))
    return cos_t, sin_t


def _layer_weights(w_in, cmp_pos, cmp_w1, cmp_w2):
    o = np.concatenate([[0], np.cumsum(IN_SIZES)])
    qa, ka, va, ga, qb, kvb, gb, nsag, qkvc, gc, qkvd, gd, merge = [int(v) for v in o[:-1]]
    cols = lambda a, w: w_in[:, a:a + w]
    hd = HEAD_DIM
    w1 = jnp.concatenate([
        cols(qa, 256), cols(ka, 256), cols(va, 256), cols(qb, 256),
        cols(kvb + 2 * hd, hd), cols(kvb + 4 * hd, hd),
        cols(kvb, 2 * hd),
        cols(kvb + 3 * hd, hd), cols(kvb + 5 * hd, hd),
        cols(qkvc, 768), cols(qkvd, 2304)], axis=1).astype(BF16)
    gate_cols = jnp.concatenate([cols(ga, 256), cols(gb, 256), cols(gc, 256), cols(gd, 256)], axis=1)
    nsag_cols = jnp.repeat(cols(nsag, 3 * N_HEADS), HEAD_DIM, axis=1)
    w2 = jnp.concatenate([gate_cols, cols(merge, 4096), nsag_cols], axis=1).astype(BF16)

    half = B_CMP_STRIDE
    pe = jnp.concatenate([cmp_pos[0], cmp_pos[1]], axis=-1)
    pe_lo = pe[:half].reshape(1, half * 128)
    pe_hi = pe[half:].reshape(1, half * 128)
    w1k = cmp_w1[0].reshape(B_CMP_LEN, hd, B_CMP_HIDDEN)
    w1v = cmp_w1[1].reshape(B_CMP_LEN, hd, B_CMP_HIDDEN)
    zw = jnp.zeros_like(w1k)
    w1_big = jnp.concatenate([jnp.concatenate([w1k, zw], axis=2),
                              jnp.concatenate([zw, w1v], axis=2)], axis=1)
    w_lo = w1_big[:half].reshape(half * 128, 2 * B_CMP_HIDDEN).astype(BF16)
    w_hi = w1_big[half:].reshape(half * 128, 2 * B_CMP_HIDDEN).astype(BF16)

    def placed(w2h, top):
        blocks = []
        for h in range(N_HEADS):
            blk = jnp.zeros((2 * B_CMP_HIDDEN, 256), F32)
            r0 = 0 if top else B_CMP_HIDDEN
            blk = blk.at[r0:r0 + B_CMP_HIDDEN, h * hd:(h + 1) * hd].set(w2h)
            blocks.append(blk)
        return jnp.concatenate(blocks, axis=1).astype(BF16)

    return w1, w2, pe_lo, pe_hi, w_lo, w_hi, placed(cmp_w2[0], True), placed(cmp_w2[1], False)


def _overlap_t(s):
    nr = s // B_CMP_STRIDE
    nc = nr - 1
    nsel = s // B_SLC_LEN
    starts = np.arange(nr) * B_CMP_STRIDE
    j = np.arange(nsel)
    ov = ((starts[None, :] < (j[:, None] + 1) * B_SLC_LEN)
          & (starts[None, :] + B_CMP_LEN > j[:, None] * B_SLC_LEN)
          & (np.arange(nr)[None, :] < nc))
    return jnp.asarray(ov.astype(np.float32)).astype(BF16)


def _layer(x, norm_w, w_in, cmp_pos, cmp_w1, cmp_w2, w_up, w_out, final_norm_w, final, tables, ovt):
    b, s, _ = x.shape
    w1, w2, pe_lo, pe_hi, w_lo, w_hi, w2k, w2v = _layer_weights(w_in, cmp_pos, cmp_w1, cmp_w2)
    nw = norm_w.reshape(1, -1)
    (qat, ka, vat, kmean, qb, qnt, ksa, kwp, vst, vwt, kcvc, qc, kc, vc, *qkvd) = _inproj(x, nw, w1, *tables)

    o_a = _moba(qat, ka, vat, kmean.transpose(0, 2, 1, 3))
    o_cmp, qbias = _nsa_cmp(kcvc, qb, pe_lo, pe_hi, w_lo, w_hi, w2k, w2v, ovt)
    o_slc, o_win = _nsa_slc_win(qnt, qbias, ksa, kwp, vst, vwt)
    o_c = _stickbreak(qc, kc, vc)

    od, ls = [], []
    for g, (window, dil) in enumerate(D_PATTERNS):
        q, k, v = (a.reshape(b * dil, s // dil, 256) for a in qkvd[3 * g:3 * g + 3])
        o, l = _dilated_group(q, k, v, window // dil)
        od.append(o.reshape(b, dil, s // dil, 256))
        ls.append(l.reshape(b, dil, s // dil, 256))

    return _merge_out(x, nw, w2, (o_a, o_cmp, o_slc, o_win, o_c), (*od, *ls),
                      w_up.astype(BF16), w_out.astype(BF16), final_norm_w.reshape(1, -1), final)


def kernel(x, norm_w, w_in, nsa_cmp_pos, nsa_cmp_w1, nsa_cmp_w2, w_up, w_out, final_norm_w):
    depth = norm_w.shape[0]
    s = x.shape[1]
    tables = _rope_tables(s)
    ovt = _overlap_t(s)
    for layer in range(depth):
        x = _layer(x, norm_w[layer], w_in[layer], nsa_cmp_pos[layer], nsa_cmp_w1[layer],
                   nsa_cmp_w2[layer], w_up[layer], w_out[layer], final_norm_w,
                   layer == depth - 1, tables, ovt)
    return x
```

```python
import functools

import numpy as np
import jax
import jax.numpy as jnp
from jax import lax
from jax.experimental import pallas as pl
from jax.experimental.pallas import tpu as pltpu

F32 = jnp.float32
BF16 = jnp.bfloat16

HEAD_DIM = 64
N_HEADS = 4
BRANCH_W = N_HEADS * HEAD_DIM
ROPE_THETA = 10000.0
NORM_EPS = 1e-6
QK_SCALE = HEAD_DIM ** -0.5

A_BLOCK = 256
A_SHIFT = 8
A_TOPK = 3
B_CMP_LEN = 32
B_CMP_STRIDE = 16
B_CMP_HIDDEN = 256
B_SLC_LEN = 64
B_SLC_SHIFT = 6
B_SLC_TOPN = 16
B_WINDOW = 512
D_PATTERNS = ((128, 1), (512, 4), (2048, 16))
N_BRANCH = 4

IN_SIZES = (
    BRANCH_W, BRANCH_W, BRANCH_W, BRANCH_W,
    BRANCH_W, 6 * HEAD_DIM, BRANCH_W, 3 * N_HEADS,
    3 * BRANCH_W, BRANCH_W,
    3 * len(D_PATTERNS) * N_HEADS * HEAD_DIM, BRANCH_W,
    N_BRANCH * 1024,
)

M_INIT = -1e30
MASKED = -2e30
SB_DEAD = -120.0
VMEM_LIMIT = 56 * 1024 * 1024
LANES = 128

TOK_TILE = 256
DENSE_TQ = 128


def _dot(a, b):
    return jnp.dot(a, b, preferred_element_type=F32)


def _dot_nt(a, b):
    return lax.dot_general(a, b, (((1,), (1,)), ((), ())), preferred_element_type=F32)


def _params(*sem):
    return pltpu.CompilerParams(dimension_semantics=sem, vmem_limit_bytes=VMEM_LIMIT)


def _rmsnorm(x, w):
    y = x * lax.rsqrt(jnp.mean(x * x, axis=-1, keepdims=True) + NORM_EPS)
    return y * w


def _head_mask(shape):
    return lax.broadcasted_iota(jnp.int32, shape, 1) >> 6


def _keep_head(q, head, h):
    return jnp.where(head == h, q.astype(F32), 0.0).astype(BF16)


def _with_mask_rows(q_t, bias_t):
    n, tq = bias_t.shape
    parts = [q_t[:HEAD_DIM], bias_t.astype(BF16)]
    if HEAD_DIM + n < LANES:
        parts.append(jnp.zeros((LANES - HEAD_DIM - n, tq), BF16))
    return jnp.concatenate(parts, axis=0)


def _heads_to_tokens(outs_t):
    return jnp.transpose(jnp.concatenate(outs_t, axis=0))


def _two_pass_scratch(n_tiles, tq):
    return [pltpu.VMEM((N_HEADS, n_tiles, tq, tq), F32), pltpu.VMEM((N_HEADS, HEAD_DIM, tq), F32),
            pltpu.VMEM((N_HEADS, 8, tq), F32), pltpu.VMEM((N_HEADS, 8, tq), F32)]


def _attend_two_pass(n_full, scores_of, values_of, keep_last, s_ref, acc_ref, peak_ref, total_ref):
    heads, _, tk, tq = s_ref.shape

    def fold(x, op):
        return op(x.reshape(tk // 8, 8, tq), axis=0)

    def for_tiles(count, fn):
        def pair(i, carry):
            fn(2 * i)
            fn(2 * i + 1)
            return carry

        lax.fori_loop(0, count // 2, pair, 0)

        @pl.when(count % 2 == 1)
        def _():
            fn(count - 1)

    def park(j, scores):
        for h, s in enumerate(scores):
            s_ref[h, j] = s
            peak_ref[h] = jnp.maximum(peak_ref[h], fold(s, jnp.max))

    peak_ref[...] = jnp.full(peak_ref.shape, MASKED, F32)
    for_tiles(n_full, lambda j: park(j, scores_of(j)))
    park(n_full, [jnp.where(keep_last, s, MASKED) for s in scores_of(n_full)])
    top = [jnp.max(peak_ref[h], axis=0, keepdims=True) for h in range(heads)]

    acc_ref[...] = jnp.zeros(acc_ref.shape, F32)
    total_ref[...] = jnp.zeros(total_ref.shape, F32)

    def weigh(j):
        for h, v_t in enumerate(values_of(j)):
            p = jnp.exp(s_ref[h, j] - top[h])
            total_ref[h] += fold(p, jnp.sum)
            acc_ref[h] += _dot(v_t, p.astype(BF16))

    for_tiles(n_full + 1, weigh)
    return [acc_ref[h] / jnp.sum(total_ref[h], axis=0, keepdims=True) for h in range(heads)]


def _softmax_once(qs, ks, vs, keeps):
    scores = [jnp.where(keep, _dot_nt(q, k), MASKED) for q, k, keep in zip(qs, ks, keeps)]
    outs, maxes, sums = [], [], []
    for s, v in zip(scores, vs):
        m = jnp.max(s, axis=1, keepdims=True)
        p = jnp.exp(s - m)
        l = jnp.sum(p, axis=1, keepdims=True)
        outs.append(_dot(p.astype(BF16), v) / l)
        maxes.append(m)
        sums.append(l)
    return outs, maxes, sums


_C_QA, _C_KA, _C_VA, _C_QB = 0, 256, 512, 768
_C_KSKW, _C_KCVC, _C_VSVW = 1024, 1152, 1280
_C_QC, _C_KC, _C_VC = 1408, 1664, 1920
_C_QD, _C_KD, _C_VD = 2176, 2944, 3712
_W1_COLS = 4480


def _inproj_kernel(x_ref, nw_ref, w_ref, cos_ref, sin_ref,
                   qat_ref, ka_ref, vat_ref, kmean_ref,
                   qb_ref, qnt_ref, ksa_ref, kwp_ref, vst_ref, vwt_ref, kcvc_ref,
                   qc_ref, kc_ref, vc_ref,
                   qd0_ref, kd0_ref, vd0_ref, qd1_ref, kd1_ref, vd1_ref, qd2_ref, kd2_ref, vd2_ref,
                   sub_ref):
    ts = x_ref.shape[1]
    pos0 = pl.program_id(1) * ts
    h = _rmsnorm(x_ref[0], nw_ref[...]).astype(BF16)
    cos = cos_ref[...]
    sin = sin_ref[...]
    lane = lax.broadcasted_iota(jnp.int32, (ts, 256), 1)
    first = (lane & 63) < 32
    lane128 = lax.broadcasted_iota(jnp.int32, (ts, 128), 1)
    pos128 = pos0 + lax.broadcasted_iota(jnp.int32, (ts, 128), 0)
    first128 = (lane128 & 63) < 32
    lo128 = lane128 < 64

    def proj(c0, w=256):
        return _dot(h, w_ref[:, c0:c0 + w])

    def rope(v):
        partner = jnp.where(first, pltpu.roll(v, 224, 1), pltpu.roll(v, 32, 1))
        return v * cos + partner * sin

    def split_heads(v):
        out = []
        for pair in range(2):
            p = v[:, pair * 128:(pair + 1) * 128]
            out.append(jnp.where(lo128, p, 0.0))
            out.append(jnp.where(lo128, pltpu.roll(p, 64, 1), 0.0))
        return out

    def store_q_t(ref, v):
        v_t = jnp.transpose(v)
        for hd in range(N_HEADS):
            ref[0, hd, 0:HEAD_DIM, :] = v_t[hd * HEAD_DIM:(hd + 1) * HEAD_DIM].astype(BF16)
            ref[0, hd, HEAD_DIM:LANES, :] = jnp.zeros((LANES - HEAD_DIM, ts), BF16)

    store_q_t(qat_ref, rope(proj(_C_QA)) * QK_SCALE)
    ka = split_heads(rope(proj(_C_KA)))
    va_t = jnp.transpose(proj(_C_VA))
    block_hot = jnp.where(lane128 - 64 == (pos128 >> A_SHIFT), 1.0, 0.0)
    for hd in range(N_HEADS):
        ka_ref[0, hd] = (ka[hd] + block_hot).astype(BF16)
        vat_ref[0, hd, 0] = va_t[hd * HEAD_DIM:(hd + 1) * HEAD_DIM].astype(BF16)
        kmean_ref[0, 0, pl.ds(hd, 1), :] = jnp.sum(ka[hd], axis=0, keepdims=True) * (1.0 / ts)

    qb = proj(_C_QB)
    qb_ref[0] = (qb * QK_SCALE).astype(BF16)
    store_q_t(qnt_ref, rope(qb) * QK_SCALE)
    kskw = proj(_C_KSKW, 128)
    partner = jnp.where(first128, pltpu.roll(kskw, 96, 1), pltpu.roll(kskw, 32, 1))
    kskw = kskw * cos[:, :128] + partner * sin[:, :128]
    slc_hot = jnp.where(lane128 - 64 == (pos128 >> B_SLC_SHIFT), 1.0, 0.0)
    ksa_ref[0] = jnp.where(lo128, kskw, slc_hot).astype(BF16)
    kwp_ref[0] = jnp.where(lo128, pltpu.roll(kskw, 64, 1), 0.0).astype(BF16)
    vsvw_t = jnp.transpose(proj(_C_VSVW, 128))
    vst_ref[0, 0] = vsvw_t[0:HEAD_DIM].astype(BF16)
    vwt_ref[0, 0] = vsvw_t[HEAD_DIM:2 * HEAD_DIM].astype(BF16)
    sub_ref[0] = proj(_C_KCVC, 128)
    for t in range(B_CMP_STRIDE):
        kcvc_ref[0, :, t * 128:(t + 1) * 128] = sub_ref[0, pl.ds(t, ts // B_CMP_STRIDE, stride=B_CMP_STRIDE), :]

    qc_ref[0] = (proj(_C_QC) * QK_SCALE).astype(BF16)
    kc_ref[0] = proj(_C_KC).astype(BF16)
    vc_ref[0] = proj(_C_VC).astype(BF16)

    def store_sub(ref, v, dil):
        if dil == 1:
            ref[0, 0] = v.astype(BF16)
            return
        n = ts // dil
        for half in range(2):
            sub_ref[half] = v[:, half * 128:(half + 1) * 128]
        for c in range(dil):
            for half in range(2):
                ref[0, c, :, half * 128:(half + 1) * 128] = (
                    sub_ref[half, pl.ds(c, n, stride=dil), :].astype(BF16))

    groups = ((qd0_ref, kd0_ref, vd0_ref), (qd1_ref, kd1_ref, vd1_ref), (qd2_ref, kd2_ref, vd2_ref))
    for g, (q_ref, k_ref, v_ref) in enumerate(groups):
        dil = D_PATTERNS[g][1]
        store_sub(q_ref, rope(proj(_C_QD + 256 * g)) * QK_SCALE, dil)
        store_sub(k_ref, rope(proj(_C_KD + 256 * g)), dil)
        store_sub(v_ref, proj(_C_VD + 256 * g), dil)


def _inproj(x, norm_w, w1, cos_t, sin_t):
    b, s, d = x.shape
    ts = TOK_TILE
    nt = s // ts
    tok = lambda w, dt: jax.ShapeDtypeStruct((b, s, w), dt)
    tok_spec = lambda w: pl.BlockSpec((1, ts, w), lambda i, j: (i, j, 0))
    head = jax.ShapeDtypeStruct((b, N_HEADS, s, 128), BF16)
    head_spec = pl.BlockSpec((1, N_HEADS, ts, 128), lambda i, j: (i, 0, j, 0))
    q_t = jax.ShapeDtypeStruct((b, N_HEADS, LANES, s), BF16)
    q_t_spec = pl.BlockSpec((1, N_HEADS, LANES, ts), lambda i, j: (i, 0, 0, j))
    v_t = jax.ShapeDtypeStruct((b, nt, HEAD_DIM, ts), BF16)
    v_t_spec = pl.BlockSpec((1, 1, HEAD_DIM, ts), lambda i, j: (i, j, 0, 0))
    nrow = ts // B_CMP_STRIDE
    out_shape = [q_t, head, jax.ShapeDtypeStruct((b, N_HEADS, nt, HEAD_DIM, ts), BF16),
                 jax.ShapeDtypeStruct((b, nt, N_HEADS, 128), F32),
                 tok(256, BF16), q_t, tok(128, BF16), tok(128, BF16), v_t, v_t,
                 jax.ShapeDtypeStruct((b, s // B_CMP_STRIDE, B_CMP_STRIDE * 128), F32),
                 tok(256, BF16), tok(256, BF16), tok(256, BF16)]
    out_specs = [q_t_spec, head_spec,
                 pl.BlockSpec((1, N_HEADS, 1, HEAD_DIM, ts), lambda i, j: (i, 0, j, 0, 0)),
                 pl.BlockSpec((1, 1, N_HEADS, 128), lambda i, j: (i, j, 0, 0)),
                 tok_spec(256), q_t_spec, tok_spec(128), tok_spec(128), v_t_spec, v_t_spec,
                 pl.BlockSpec((1, nrow, B_CMP_STRIDE * 128), lambda i, j: (i, j, 0)),
                 tok_spec(256), tok_spec(256), tok_spec(256)]
    for _, dil in D_PATTERNS:
        out_shape += [jax.ShapeDtypeStruct((b, dil, s // dil, 256), BF16)] * 3
        out_specs += [pl.BlockSpec((1, dil, ts // dil, 256), lambda i, j: (i, 0, j, 0))] * 3
    return pl.pallas_call(
        _inproj_kernel,
        grid=(b, nt),
        in_specs=[
            pl.BlockSpec((1, ts, d), lambda i, j: (i, j, 0)),
            pl.BlockSpec((1, d), lambda i, j: (0, 0)),
            pl.BlockSpec((d, _W1_COLS), lambda i, j: (0, 0)),
            pl.BlockSpec((ts, 256), lambda i, j: (j, 0)),
            pl.BlockSpec((ts, 256), lambda i, j: (j, 0)),
        ],
        out_specs=out_specs,
        out_shape=out_shape,
        scratch_shapes=[pltpu.VMEM((2, ts, 128), F32)],
        compiler_params=_params("parallel", "arbitrary"),
        name="inproj",
    )(x, norm_w, w1, cos_t, sin_t)


def _moba_kernel(qt_ref, k_ref, vt_ref, km_ref, o_ref, qc_ref, *two_pass_refs, nb, topk):
    tq = qt_ref.shape[3]
    tk = tq
    qb = pl.program_id(1)
    blk_t = lax.broadcasted_iota(jnp.int32, (nb, tq), 0)
    past_t = blk_t < qb
    key = lax.broadcasted_iota(jnp.int32, (tk, tq), 0)
    qry = lax.broadcasted_iota(jnp.int32, (tk, tq), 1)
    causal = key <= qry

    for h in range(N_HEADS):
        q_t = qt_ref[0, h]
        gate = jnp.where(past_t, _dot(km_ref[0, h].astype(BF16), q_t), -jnp.inf)
        rank = jnp.zeros((nb, tq), jnp.int32)
        for i in range(nb):
            row = gate[i:i + 1, :]
            beats = (row > gate) | ((row == gate) & (blk_t > i))
            rank = rank + jnp.where(beats, 1, 0)
        keep = ((rank < topk) & past_t) | (blk_t == qb)
        qc_ref[h] = _with_mask_rows(q_t, jnp.where(keep, 0.0, MASKED))

    def scores_of(j):
        off = pl.multiple_of(j * tk, tk)
        return [_dot(k_ref[0, h, pl.ds(off, tk), :], qc_ref[h]) for h in range(N_HEADS)]

    def values_of(j):
        return [vt_ref[0, h, j] for h in range(N_HEADS)]

    o_ref[0] = _heads_to_tokens(_attend_two_pass(qb, scores_of, values_of, causal, *two_pass_refs))


def _moba(qat, ka, vat, kmean):
    b, _, s, _ = ka.shape
    tq = A_BLOCK
    nb = s // A_BLOCK
    kern = functools.partial(_moba_kernel, nb=nb, topk=min(A_TOPK, nb))
    return pl.pallas_call(
        kern,
        grid=(b, nb),
        in_specs=[pl.BlockSpec((1, N_HEADS, LANES, tq), lambda i, j: (i, 0, 0, j)),
                  pl.BlockSpec((1, N_HEADS, s, 128), lambda i, j: (i, 0, 0, 0)),
                  pl.BlockSpec((1, N_HEADS, nb, HEAD_DIM, tq), lambda i, j: (i, 0, 0, 0, 0)),
                  pl.BlockSpec((1, N_HEADS, nb, 128), lambda i, j: (i, 0, 0, 0))],
        out_specs=pl.BlockSpec((1, tq, 256), lambda i, j: (i, j, 0)),
        out_shape=jax.ShapeDtypeStruct((b, s, 256), F32),
        scratch_shapes=[pltpu.VMEM((N_HEADS, LANES, tq), BF16)] + _two_pass_scratch(nb, tq),
        compiler_params=_params("parallel", "arbitrary"),
        name="moba",
    )(qat, ka, vat, kmean)


def _gelu_tanh(x):
    return 0.5 * x * (1.0 + jnp.tanh(np.sqrt(2.0 / np.pi).astype(np.float32) * (x + 0.044715 * (x * x * x))))


def _cmp_kernel(xr_ref, pelo_ref, pehi_ref, wlo_ref, whi_ref, w2k_ref, w2v_ref, q_ref, ovt_ref,
                o_ref, qbias_ref, kbig_ref, vbig_ref, *, nsel, topn):
    tq = q_ref.shape[1]
    nr = xr_ref.shape[1]
    nc = nr - 1
    qi = pl.program_id(1)

    @pl.when(qi == 0)
    def _():
        xr = xr_ref[0]
        lo = (xr + pelo_ref[...]).astype(BF16)
        hi = (pltpu.roll(xr, nr - 1, 0) + pehi_ref[...]).astype(BF16)
        hid = _dot(lo, wlo_ref[...]) + _dot(hi, whi_ref[...])
        act = _gelu_tanh(hid).astype(BF16)
        kbig_ref[...] = _dot(act, w2k_ref[...]).astype(BF16)
        vbig_ref[...] = _dot(act, w2v_ref[...]).astype(BF16)

    q = q_ref[0]
    head = _head_mask((tq, 256))
    tpos = qi * tq + lax.broadcasted_iota(jnp.int32, (tq, nr), 0)
    n = lax.broadcasted_iota(jnp.int32, (tq, nr), 1)
    vis = (n * B_CMP_STRIDE + (B_CMP_LEN - 1) <= tpos) & (n < nc)
    ovt = ovt_ref[...]
    out = jnp.zeros((tq, 256), F32)
    imp = jnp.zeros((nsel, tq), F32)
    for h in range(N_HEADS):
        qh = _keep_head(q, head, h)
        s = jnp.where(vis, _dot_nt(qh, kbig_ref[:, h * 256:(h + 1) * 256]), MASKED)
        mx = jnp.max(s, axis=1, keepdims=True)
        e = jnp.where(vis, jnp.exp(s - mx), 0.0)
        den = jnp.maximum(jnp.sum(e, axis=1, keepdims=True), 1e-30)
        p = (e / den).astype(BF16)
        out = out + _dot(p, vbig_ref[:, h * 256:(h + 1) * 256])
        imp = imp + _dot_nt(ovt, p)
    o_ref[0] = out

    jj = lax.broadcasted_iota(jnp.int32, (nsel, tq), 0)
    cur = (qi * tq + lax.broadcasted_iota(jnp.int32, (nsel, tq), 1)) >> B_SLC_SHIFT
    forced = (jj == 0) | (jj == cur) | (jj == cur - 1)
    valid = jj <= cur
    imp = jnp.where(valid, jnp.where(forced, jnp.inf, imp), -jnp.inf)
    rank = jnp.zeros((nsel, tq), jnp.int32)
    for i in range(nsel):
        row = imp[i:i + 1, :]
        beats = (row > imp) | ((row == imp) & (jj > i))
        rank = rank + jnp.where(beats, 1, 0)
    qbias_ref[0] = jnp.where((rank < topn) & valid, 0.0, MASKED).astype(BF16)


def _nsa_cmp(xr, qb, pe_lo, pe_hi, w_lo, w_hi, w2k, w2v, ovt):
    b, s, _ = qb.shape
    nr = s // B_CMP_STRIDE
    nsel = s // B_SLC_LEN
    assert nsel <= LANES - HEAD_DIM, "the selection mask lives in the spare contraction rows of q"
    tq = TOK_TILE
    kern = functools.partial(_cmp_kernel, nsel=nsel, topn=min(B_SLC_TOPN, nsel))
    full = lambda a: pl.BlockSpec(a.shape, lambda i, j: (0,) * a.ndim)
    return pl.pallas_call(
        kern,
        grid=(b, s // tq),
        in_specs=[pl.BlockSpec((1, nr, B_CMP_STRIDE * 128), lambda i, j: (i, 0, 0)),
                  full(pe_lo), full(pe_hi), full(w_lo), full(w_hi), full(w2k), full(w2v),
                  pl.BlockSpec((1, tq, 256), lambda i, j: (i, j, 0)), full(ovt)],
        out_specs=[pl.BlockSpec((1, tq, 256), lambda i, j: (i, j, 0)),
                   pl.BlockSpec((1, nsel, tq), lambda i, j: (i, 0, j))],
        out_shape=[jax.ShapeDtypeStruct((b, s, 256), F32),
                   jax.ShapeDtypeStruct((b, nsel, s), BF16)],
        scratch_shapes=[pltpu.VMEM((nr, 1024), BF16), pltpu.VMEM((nr, 1024), BF16)],
        compiler_params=_params("parallel", "arbitrary"),
        name="nsa_cmp",
    )(xr, pe_lo, pe_hi, w_lo, w_hi, w2k, w2v, qb, ovt)


def _nsa_kernel(qt_ref, qbias_ref, ksa_ref, kwp_ref, vst_ref, vwt_ref, oslc_ref, owin_ref,
                qc_ref, *two_pass_refs, ntile):
    tq = qt_ref.shape[3]
    tk = tq
    qi = pl.program_id(1)
    q_t = [qt_ref[0, h] for h in range(N_HEADS)]
    key = lax.broadcasted_iota(jnp.int32, (tk, tq), 0)
    qry = lax.broadcasted_iota(jnp.int32, (tk, tq), 1)

    bias_t = qbias_ref[0]
    for h in range(N_HEADS):
        qc_ref[h] = _with_mask_rows(q_t[h], bias_t)

    def scores_of(j):
        kt = ksa_ref[0, pl.ds(pl.multiple_of(j * tk, tk), tk), :]
        return [_dot(kt, qc_ref[h]) for h in range(N_HEADS)]

    def values_of(j):
        return [vst_ref[0, j]] * N_HEADS

    oslc_ref[0] = _heads_to_tokens(
        _attend_two_pass(qi, scores_of, values_of, key <= qry, *two_pass_refs))

    t0 = jnp.maximum(qi + 1 - ntile, 0)
    kt = kwp_ref[0, pl.ds(pl.multiple_of(t0 * tk, tk), ntile * tk), :]
    span = ntile * tk
    dist = (qi * tq + lax.broadcasted_iota(jnp.int32, (span, tq), 1)) - (
        t0 * tk + lax.broadcasted_iota(jnp.int32, (span, tq), 0))
    keep = (dist >= 0) & (dist < B_WINDOW)
    scores = [jnp.where(keep, _dot(kt, q_t[h]), MASKED) for h in range(N_HEADS)]
    outs = []
    for s in scores:
        m = jnp.max(s, axis=0, keepdims=True)
        p = jnp.exp(s - m)
        l = jnp.sum(p, axis=0, keepdims=True)
        p = p.astype(BF16)
        o = _dot(vwt_ref[0, t0], p[0:tk])
        for i in range(1, ntile):
            o = o + _dot(vwt_ref[0, t0 + i], p[i * tk:(i + 1) * tk])
        outs.append(o / l)
    owin_ref[0] = _heads_to_tokens(outs)


def _nsa_slc_win(qnt, qbias, ksa, kwp, vst, vwt):
    b, s, _ = ksa.shape
    tq = TOK_TILE
    nt = s // tq
    nsel = s // B_SLC_LEN
    ntile = min(B_WINDOW // tq + 1, nt)
    seq = pl.BlockSpec((1, s, 128), lambda i, j: (i, 0, 0))
    v_t = pl.BlockSpec((1, nt, HEAD_DIM, tq), lambda i, j: (i, 0, 0, 0))
    ospec = pl.BlockSpec((1, tq, 256), lambda i, j: (i, j, 0))
    return pl.pallas_call(
        functools.partial(_nsa_kernel, ntile=ntile),
        grid=(b, nt),
        in_specs=[pl.BlockSpec((1, N_HEADS, LANES, tq), lambda i, j: (i, 0, 0, j)),
                  pl.BlockSpec((1, nsel, tq), lambda i, j: (i, 0, j)), seq, seq, v_t, v_t],
        out_specs=[ospec, ospec],
        out_shape=[jax.ShapeDtypeStruct((b, s, 256), F32)] * 2,
        scratch_shapes=[pltpu.VMEM((N_HEADS, LANES, tq), BF16)] + _two_pass_scratch(nt, tq),
        compiler_params=_params("parallel", "arbitrary"),
        name="nsa_slc_win",
    )(qnt, qbias, ksa, kwp, vst, vwt)


def _sb_kernel(q_ref, k_ref, v_ref, o_ref, c_ref, acc_ref):
    tq = q_ref.shape[1]
    tk = tq
    qi = pl.program_id(1)
    q = q_ref[0]
    head = _head_mask((tq, 256))
    qh = [_keep_head(q, head, h) for h in range(N_HEADS)]
    rows = lax.broadcasted_iota(jnp.int32, (tq, tk), 0)
    cols = lax.broadcasted_iota(jnp.int32, (tq, tk), 1)
    after = jnp.where(rows > cols, 1.0, 0.0).astype(BF16)
    after2 = jnp.concatenate([after, after], axis=0)
    c_ref[...] = jnp.zeros(c_ref.shape, F32)
    acc_ref[...] = jnp.zeros(acc_ref.shape, F32)

    def tile(j, strict_causal):
        off = pl.multiple_of(j * tk, tk)
        kt = k_ref[0, pl.ds(off, tk), :]
        vt = v_ref[0, pl.ds(off, tk), :]
        zs = [_dot_nt(qh[h], kt) for h in range(N_HEADS)]
        log_betas, log_1ms, tails = [], [], []
        for z in zs:
            log_beta = jnp.minimum(z, 0.0) - jnp.log(1.0 + jnp.exp(-jnp.abs(z)))
            log_1m = log_beta - z
            if strict_causal is not None:
                log_1m = jnp.where(strict_causal, log_1m, 0.0)
            hi = log_1m.astype(BF16)
            lo = (log_1m - hi.astype(F32)).astype(BF16)
            tails.append(_dot(jnp.concatenate([hi, lo], axis=1), after2))
            log_betas.append(log_beta)
            log_1ms.append(log_1m)
        alive = None
        for h in range(N_HEADS):
            c = c_ref[h]
            a = jnp.exp(log_betas[h] + (tails[h] + c))
            if strict_causal is not None:
                a = jnp.where(strict_causal, a, 0.0)
            acc_ref[h] += _dot(a.astype(BF16), vt)
            c_new = c + jnp.sum(log_1ms[h], axis=1, keepdims=True)
            c_ref[h] = c_new
            alive = c_new if alive is None else jnp.maximum(alive, c_new)
        return jnp.max(alive)

    def cond(carry):
        j, cmax = carry
        return (j >= 1) & (cmax > SB_DEAD)

    def body(carry):
        j, _ = carry
        tile(j, None)
        return j - 2, tile(j - 1, None)

    j, cmax = lax.while_loop(cond, body, (qi - 1, tile(qi, cols < rows)))

    @pl.when((j == 0) & (cmax > SB_DEAD))
    def _():
        tile(0, None)
    out = acc_ref[0]
    for h in range(1, N_HEADS):
        out = jnp.where(head == h, acc_ref[h], out)
    o_ref[0] = out


def _stickbreak(qc, kc, vc):
    b, s, _ = qc.shape
    tq = DENSE_TQ
    seq = pl.BlockSpec((1, s, 256), lambda i, j: (i, 0, 0))
    tile = pl.BlockSpec((1, tq, 256), lambda i, j: (i, j, 0))
    return pl.pallas_call(
        _sb_kernel,
        grid=(b, s // tq),
        in_specs=[tile, seq, seq],
        out_specs=tile,
        out_shape=jax.ShapeDtypeStruct((b, s, 256), F32),
        scratch_shapes=[pltpu.VMEM((N_HEADS, tq, LANES), F32), pltpu.VMEM((N_HEADS, tq, 256), F32)],
        compiler_params=_params("parallel", "arbitrary"),
        name="stickbreak",
    )(qc, kc, vc)


def _dil_kernel(q_ref, k_ref, v_ref, o_ref, lse_ref, *, window, span, tq):
    nq = q_ref.shape[1] // tq
    head = _head_mask((tq, 256))
    rows = lax.broadcasted_iota(jnp.int32, (tq, span), 0)
    cols = lax.broadcasted_iota(jnp.int32, (tq, span), 1)
    qs, ks, vs, keeps = [], [], [], []
    for t in range(nq):
        q0 = (pl.program_id(1) * nq + t) * tq
        k0 = pl.multiple_of(jnp.maximum(q0 + tq - span, 0), tq)
        q = q_ref[0, t * tq:(t + 1) * tq, :]
        dist = (q0 + rows) - (k0 + cols)
        qs += [_keep_head(q, head, h) for h in range(N_HEADS)]
        ks += [k_ref[0, pl.ds(k0, span), :]] * N_HEADS
        vs += [v_ref[0, pl.ds(k0, span), :]] * N_HEADS
        keeps += [(dist >= 0) & (dist <= window)] * N_HEADS
    o, m, l = _softmax_once(qs, ks, vs, keeps)
    for t in range(nq):
        out = o[t * N_HEADS]
        lse = jnp.broadcast_to(m[t * N_HEADS] + jnp.log(l[t * N_HEADS]), (tq, 256))
        for h in range(1, N_HEADS):
            i = t * N_HEADS + h
            out = jnp.where(head == h, o[i], out)
            lse = jnp.where(head == h, m[i] + jnp.log(l[i]), lse)
        o_ref[0, t * tq:(t + 1) * tq, :] = out
        lse_ref[0, t * tq:(t + 1) * tq, :] = lse


def _dilated_group(q, k, v, window):
    n, L, _ = q.shape
    tq = DENSE_TQ
    assert window == tq
    nq = 2 if L % (2 * tq) == 0 else 1
    kern = functools.partial(_dil_kernel, window=window, span=min(2 * tq, L), tq=tq)
    seq = pl.BlockSpec((1, L, 256), lambda i, j: (i, 0, 0))
    tile = pl.BlockSpec((1, nq * tq, 256), lambda i, j: (i, j, 0))
    return pl.pallas_call(
        kern,
        grid=(n, L // (nq * tq)),
        in_specs=[tile, seq, seq],
        out_specs=[tile, tile],
        out_shape=[jax.ShapeDtypeStruct((n, L, 256), F32)] * 2,
        compiler_params=_params("parallel", "arbitrary"),
        name="dilated",
    )(q, k, v)


_C_GATE, _C_MERGE, _C_NSAG = 0, 1024, 5120
_W2_COLS = 5248


def _out_kernel(x_ref, nw_ref, w2_ref, oa_ref, ocmp_ref, oslc_ref, owin_ref, oc_ref,
                od0_ref, od1_ref, od2_ref, ls0_ref, ls1_ref, ls2_ref,
                wup_ref, wout_ref, fnw_ref, spread_ref, y_ref, acc_ref, sub_ref, *, final):
    ts = x_ref.shape[1]
    x = x_ref[0]
    h = _rmsnorm(x, nw_ref[...]).astype(BF16)

    def proj(c0, w):
        return _dot(h, w2_ref[:, c0:c0 + w])

    def token_order(ref, dil, slot):
        if dil == 1:
            return ref[0, 0]
        n = ts // dil
        for c in range(dil):
            for half in range(2):
                sub_ref[slot, half, pl.ds(c, n, stride=dil), :] = ref[0, c, :, half * 128:(half + 1) * 128]
        return jnp.concatenate([sub_ref[slot, 0], sub_ref[slot, 1]], axis=1)

    gate = jax.nn.sigmoid(proj(_C_NSAG, 128))
    hi = gate.astype(BF16)
    rest = gate - hi.astype(F32)
    mid = rest.astype(BF16)
    lo = (rest - mid.astype(F32)).astype(BF16)
    spread = spread_ref[...]
    ng = _dot(hi, spread) + _dot(mid, spread) + _dot(lo, spread)
    o_b = ng[:, 0:256] * ocmp_ref[0] + ng[:, 256:512] * oslc_ref[0] + ng[:, 512:768] * owin_ref[0]

    dils = [dil for _, dil in D_PATTERNS]
    od = [token_order(r, d, i) for i, (r, d) in enumerate(zip((od0_ref, od1_ref, od2_ref), dils))]
    ls = [token_order(r, d, 3 + i) for i, (r, d) in enumerate(zip((ls0_ref, ls1_ref, ls2_ref), dils))]
    mx = jnp.maximum(jnp.maximum(ls[0], ls[1]), ls[2])
    e = [jnp.exp(l - mx) for l in ls]
    den = e[0] + e[1] + e[2]
    o_d = (e[0] / den) * od[0] + (e[1] / den) * od[1] + (e[2] / den) * od[2]

    branches = (oa_ref[0], o_b, oc_ref[0], o_d)
    acc_ref[...] = jnp.zeros(acc_ref.shape, F32)
    for i in range(N_BRANCH):
        g = proj(_C_GATE + 256 * i, 256)
        wide = (branches[i] * (g * jax.nn.sigmoid(g))).astype(BF16)
        for half in range(2):
            c0 = 512 * half
            u = _dot(wide, wup_ref[i, :, c0:c0 + 512])
            mg = jax.nn.sigmoid(proj(_C_MERGE + 1024 * i + c0, 512))
            acc_ref[:, c0:c0 + 512] += mg * u
    out = x + _dot(acc_ref[...].astype(BF16), wout_ref[...])
    if final:
        out = _rmsnorm(out, fnw_ref[...])
    y_ref[0] = out


def _merge_out(x, norm_w, w2, token_outs, sub_outs, w_up, w_out, final_norm_w, final):
    b, s, d = x.shape
    ts = TOK_TILE
    tile256 = pl.BlockSpec((1, ts, 256), lambda i, j: (i, j, 0))
    sub_specs = [pl.BlockSpec((1, dil, ts // dil, 256), lambda i, j: (i, 0, j, 0)) for _, dil in D_PATTERNS]
    const = lambda a: pl.BlockSpec(a.shape, lambda i, j: (0,) * a.ndim)
    kern = functools.partial(_out_kernel, final=final)
    gate_id = np.arange(3 * BRANCH_W) // HEAD_DIM
    spread = jnp.asarray(np.arange(128)[:, None] == gate_id[None, :], dtype=BF16)
    return pl.pallas_call(
        kern,
        grid=(b, s // ts),
        in_specs=[pl.BlockSpec((1, ts, d), lambda i, j: (i, j, 0)), const(norm_w), const(w2)]
        + [tile256] * 5 + sub_specs * 2
        + [const(w_up), const(w_out), const(final_norm_w), const(spread)],
        out_specs=pl.BlockSpec((1, ts, d), lambda i, j: (i, j, 0)),
        out_shape=jax.ShapeDtypeStruct((b, s, d), F32),
        scratch_shapes=[pltpu.VMEM((ts, d), F32), pltpu.VMEM((6, 2, ts, 128), F32)],
        compiler_params=_params("parallel", "arbitrary"),
        name="merge_out",
    )(x, norm_w, w2, *token_outs, *sub_outs, w_up, w_out, final_norm_w, spread)


def _rope_tables(s):
    half = HEAD_DIM // 2
    inv = ROPE_THETA ** (-jnp.arange(half, dtype=F32) / half)
    ang = jnp.arange(s).astype(F32)[:, None] * inv[None, :]
    cos, sin = jnp.cos(ang), jnp.sin(ang)
    cos_t = jnp.tile(jnp.concatenate([cos, cos], axis=-1), (1, N_HEADS))
    sin_t = jnp.tile(jnp.concatenate([-sin, sin], axis=-1), (1, N_HEADS))
    return cos_t, sin_t


def _layer_weights(w_in, cmp_pos, cmp_w1, cmp_w2):
    o = np.concatenate([[0], np.cumsum(IN_SIZES)])
    qa, ka, va, ga, qb, kvb, gb, nsag, qkvc, gc, qkvd, gd, merge = [int(v) for v in o[:-1]]
    cols = lambda a, w: w_in[:, a:a + w]
    hd = HEAD_DIM
    w1 = jnp.concatenate([
        cols(qa, 256), cols(ka, 256), cols(va, 256), cols(qb, 256),
        cols(kvb + 2 * hd, hd), cols(kvb + 4 * hd, hd),
        cols(kvb, 2 * hd),
        cols(kvb + 3 * hd, hd), cols(kvb + 5 * hd, hd),
        cols(qkvc, 768), cols(qkvd, 2304)], axis=1)
    gate_cols = jnp.concatenate([cols(ga, 256), cols(gb, 256), cols(gc, 256), cols(gd, 256)], axis=1)
    nsag_cols = jnp.pad(cols(nsag, 3 * N_HEADS), ((0, 0), (0, 128 - 3 * N_HEADS)))
    w2 = jnp.concatenate([gate_cols, cols(merge, 4096), nsag_cols], axis=1)

    half = B_CMP_STRIDE
    pe = jnp.concatenate([cmp_pos[0], cmp_pos[1]], axis=-1)
    pe_lo = pe[:half].reshape(1, half * 128)
    pe_hi = pe[half:].reshape(1, half * 128)
    w1k = cmp_w1[0].reshape(B_CMP_LEN, hd, B_CMP_HIDDEN)
    w1v = cmp_w1[1].reshape(B_CMP_LEN, hd, B_CMP_HIDDEN)
    zw = jnp.zeros_like(w1k)
    w1_big = jnp.concatenate([jnp.concatenate([w1k, zw], axis=2),
                              jnp.concatenate([zw, w1v], axis=2)], axis=1)
    w_lo = w1_big[:half].reshape(half * 128, 2 * B_CMP_HIDDEN).astype(BF16)
    w_hi = w1_big[half:].reshape(half * 128, 2 * B_CMP_HIDDEN).astype(BF16)

    def placed(w2h, top):
        blocks = []
        for h in range(N_HEADS):
            blk = jnp.zeros((2 * B_CMP_HIDDEN, 256), F32)
            r0 = 0 if top else B_CMP_HIDDEN
            blk = blk.at[r0:r0 + B_CMP_HIDDEN, h * hd:(h + 1) * hd].set(w2h)
            blocks.append(blk)
        return jnp.concatenate(blocks, axis=1).astype(BF16)

    return w1, w2, pe_lo, pe_hi, w_lo, w_hi, placed(cmp_w2[0], True), placed(cmp_w2[1], False)


def _overlap_t(s):
    nr = s // B_CMP_STRIDE
    nc = nr - 1
    nsel = s // B_SLC_LEN
    starts = np.arange(nr) * B_CMP_STRIDE
    j = np.arange(nsel)
    ov = ((starts[None, :] < (j[:, None] + 1) * B_SLC_LEN)
          & (starts[None, :] + B_CMP_LEN > j[:, None] * B_SLC_LEN)
          & (np.arange(nr)[None, :] < nc))
    return jnp.asarray(ov.astype(np.float32)).astype(BF16)


def _layer(x, norm_w, w_in, cmp_pos, cmp_w1, cmp_w2, w_up, w_out, final_norm_w, final, tables, ovt):
    b, s, _ = x.shape
    w1, w2, pe_lo, pe_hi, w_lo, w_hi, w2k, w2v = _layer_weights(w_in, cmp_pos, cmp_w1, cmp_w2)
    nw = norm_w.reshape(1, -1)
    (qat, ka, vat, kmean, qb, qnt, ksa, kwp, vst, vwt, kcvc, qc, kc, vc, *qkvd) = _inproj(x, nw, w1, *tables)

    o_a = _moba(qat, ka, vat, kmean.transpose(0, 2, 1, 3))
    o_cmp, qbias = _nsa_cmp(kcvc, qb, pe_lo, pe_hi, w_lo, w_hi, w2k, w2v, ovt)
    o_slc, o_win = _nsa_slc_win(qnt, qbias, ksa, kwp, vst, vwt)
    o_c = _stickbreak(qc, kc, vc)

    od, ls = [], []
    for g, (window, dil) in enumerate(D_PATTERNS):
        q, k, v = (a.reshape(b * dil, s // dil, 256) for a in qkvd[3 * g:3 * g + 3])
        o, l = _dilated_group(q, k, v, window // dil)
        od.append(o.reshape(b, dil, s // dil, 256))
        ls.append(l.reshape(b, dil, s // dil, 256))

    return _merge_out(x, nw, w2, (o_a, o_cmp, o_slc, o_win, o_c), (*od, *ls),
                      w_up.astype(BF16), w_out.astype(BF16), final_norm_w.reshape(1, -1), final)


def kernel(x, norm_w, w_in, nsa_cmp_pos, nsa_cmp_w1, nsa_cmp_w2, w_up, w_out, final_norm_w):
    depth = norm_w.shape[0]
    s = x.shape[1]
    tables = _rope_tables(s)
    ovt = _overlap_t(s)
    w_in = w_in.astype(BF16)
    for layer in range(depth):
        x = _layer(x, norm_w[layer], w_in[layer], nsa_cmp_pos[layer], nsa_cmp_w1[layer],
                   nsa_cmp_w2[layer], w_up[layer], w_out[layer], final_norm_w,
                   layer == depth - 1, tables, ovt)
    return x
```

```python
import functools

import numpy as np
import jax
import jax.numpy as jnp
from jax import lax
from jax.experimental import pallas as pl
from jax.experimental.pallas import tpu as pltpu

F32 = jnp.float32
BF16 = jnp.bfloat16

HEAD_DIM = 64
N_HEADS = 4
BRANCH_W = N_HEADS * HEAD_DIM
ROPE_THETA = 10000.0
NORM_EPS = 1e-6
QK_SCALE = HEAD_DIM ** -0.5

A_BLOCK = 256
A_SHIFT = 8
A_TOPK = 3
B_CMP_LEN = 32
B_CMP_STRIDE = 16
B_CMP_HIDDEN = 256
B_SLC_LEN = 64
B_SLC_SHIFT = 6
B_SLC_TOPN = 16
B_WINDOW = 512
D_PATTERNS = ((128, 1), (512, 4), (2048, 16))
N_BRANCH = 4

IN_SIZES = (
    BRANCH_W, BRANCH_W, BRANCH_W, BRANCH_W,
    BRANCH_W, 6 * HEAD_DIM, BRANCH_W, 3 * N_HEADS,
    3 * BRANCH_W, BRANCH_W,
    3 * len(D_PATTERNS) * N_HEADS * HEAD_DIM, BRANCH_W,
    N_BRANCH * 1024,
)

MASKED = -2e30
LOG2_E = 1.4426950408889634
SB_DEAD = -120.0
VMEM_LIMIT = 56 * 1024 * 1024
LANES = 128

TOK_TILE = 256
DENSE_TQ = 128


def _dot(a, b):
    return jnp.dot(a, b, preferred_element_type=F32)


def _dot_nt(a, b):
    return lax.dot_general(a, b, (((1,), (1,)), ((), ())), preferred_element_type=F32)


def _params(*sem):
    return pltpu.CompilerParams(dimension_semantics=sem, vmem_limit_bytes=VMEM_LIMIT)


def _rmsnorm(x, w):
    y = x * lax.rsqrt(jnp.mean(x * x, axis=-1, keepdims=True) + NORM_EPS)
    return y * w


def _head_mask(shape):
    return lax.broadcasted_iota(jnp.int32, shape, 1) >> 6


def _keep_head(q, head, h):
    return jnp.where(head == h, q.astype(F32), 0.0).astype(BF16)


def _with_mask_rows(q_t, bias_t):
    n, tq = bias_t.shape
    parts = [q_t[:HEAD_DIM], bias_t.astype(BF16)]
    if HEAD_DIM + n < LANES:
        parts.append(jnp.zeros((LANES - HEAD_DIM - n, tq), BF16))
    return jnp.concatenate(parts, axis=0)


def _heads_to_tokens(outs_t):
    return jnp.transpose(jnp.concatenate(outs_t, axis=0))


def _two_pass_scratch(n_tiles, tq):
    return [pltpu.VMEM((N_HEADS, n_tiles, tq, tq), F32), pltpu.VMEM((N_HEADS, HEAD_DIM, tq), F32),
            pltpu.VMEM((N_HEADS, 8, tq), F32), pltpu.VMEM((N_HEADS, 8, tq), F32)]


def _attend_two_pass(n_full, scores_of, values_of, keep_last, s_ref, acc_ref, peak_ref, total_ref):
    heads, _, tk, tq = s_ref.shape

    def fold(x, op):
        return op(x.reshape(tk // 8, 8, tq), axis=0)

    def for_tiles(count, fn):
        def quad(i, carry):
            for t in range(4):
                fn(4 * i + t)
            return carry

        lax.fori_loop(0, count // 4, quad, 0)
        done = (count // 4) * 4

        @pl.when((count & 2) != 0)
        def _():
            fn(done)
            fn(done + 1)

        @pl.when((count & 1) != 0)
        def _():
            fn(count - 1)

    def park(j, scores):
        for h, s in enumerate(scores):
            s = s * LOG2_E
            s_ref[h, j] = s
            peak_ref[h] = jnp.maximum(peak_ref[h], fold(s, jnp.max))

    peak_ref[...] = jnp.full(peak_ref.shape, MASKED, F32)
    for_tiles(n_full, lambda j: park(j, scores_of(j)))
    park(n_full, [jnp.where(keep_last, s, MASKED) for s in scores_of(n_full)])
    top = [jnp.max(peak_ref[h], axis=0, keepdims=True) for h in range(heads)]

    acc_ref[...] = jnp.zeros(acc_ref.shape, F32)
    total_ref[...] = jnp.zeros(total_ref.shape, F32)

    def weigh(j):
        for h, v_t in enumerate(values_of(j)):
            p = jnp.exp2(s_ref[h, j] - top[h])
            total_ref[h] += fold(p, jnp.sum)
            acc_ref[h] += _dot(v_t, p.astype(BF16))

    for_tiles(n_full + 1, weigh)
    return [acc_ref[h] / jnp.sum(total_ref[h], axis=0, keepdims=True) for h in range(heads)]


def _softmax_once(qs, ks, vs, keeps):
    scores = [jnp.where(keep, _dot_nt(q, k), MASKED) for q, k, keep in zip(qs, ks, keeps)]
    outs, maxes, sums = [], [], []
    for s, v in zip(scores, vs):
        m = jnp.max(s, axis=1, keepdims=True)
        p = jnp.exp(s - m)
        l = jnp.sum(p, axis=1, keepdims=True)
        outs.append(_dot(p.astype(BF16), v) / l)
        maxes.append(m)
        sums.append(l)
    return outs, maxes, sums


_C_QA, _C_KA, _C_VA, _C_QB = 0, 256, 512, 768
_C_KSKW, _C_KCVC, _C_VSVW = 1024, 1152, 1280
_C_QC, _C_KC, _C_VC = 1408, 1664, 1920
_C_QD, _C_KD, _C_VD = 2176, 2944, 3712
_W1_COLS = 4480


def _inproj_kernel(x_ref, nw_ref, w_ref, cos_ref, sin_ref,
                   qat_ref, ka_ref, vat_ref, kmean_ref,
                   qb_ref, qnt_ref, ksa_ref, kwp_ref, vst_ref, vwt_ref, kcvc_ref,
                   qc_ref, kc_ref, vc_ref,
                   qd0_ref, kd0_ref, vd0_ref, qd1_ref, kd1_ref, vd1_ref, qd2_ref, kd2_ref, vd2_ref,
                   sub_ref):
    ts = x_ref.shape[1]
    pos0 = pl.program_id(1) * ts
    h = _rmsnorm(x_ref[0], nw_ref[...]).astype(BF16)
    cos = cos_ref[...]
    sin = sin_ref[...]
    lane = lax.broadcasted_iota(jnp.int32, (ts, 256), 1)
    first = (lane & 63) < 32
    lane128 = lax.broadcasted_iota(jnp.int32, (ts, 128), 1)
    pos128 = pos0 + lax.broadcasted_iota(jnp.int32, (ts, 128), 0)
    first128 = (lane128 & 63) < 32
    lo128 = lane128 < 64

    def proj(c0, w=256):
        return _dot(h, w_ref[:, c0:c0 + w])

    def rope(v):
        partner = jnp.where(first, pltpu.roll(v, 224, 1), pltpu.roll(v, 32, 1))
        return v * cos + partner * sin

    def split_heads(v):
        out = []
        for pair in range(2):
            p = v[:, pair * 128:(pair + 1) * 128]
            out.append(jnp.where(lo128, p, 0.0))
            out.append(jnp.where(lo128, pltpu.roll(p, 64, 1), 0.0))
        return out

    def store_q_t(ref, v):
        v_t = jnp.transpose(v)
        for hd in range(N_HEADS):
            ref[0, hd, 0:HEAD_DIM, :] = v_t[hd * HEAD_DIM:(hd + 1) * HEAD_DIM].astype(BF16)
            ref[0, hd, HEAD_DIM:LANES, :] = jnp.zeros((LANES - HEAD_DIM, ts), BF16)

    store_q_t(qat_ref, rope(proj(_C_QA)) * QK_SCALE)
    ka = split_heads(rope(proj(_C_KA)))
    va_t = jnp.transpose(proj(_C_VA))
    block_hot = jnp.where(lane128 - 64 == (pos128 >> A_SHIFT), 1.0, 0.0)
    for hd in range(N_HEADS):
        ka_ref[0, hd] = (ka[hd] + block_hot).astype(BF16)
        vat_ref[0, hd, 0] = va_t[hd * HEAD_DIM:(hd + 1) * HEAD_DIM].astype(BF16)
        kmean_ref[0, 0, pl.ds(hd, 1), :] = jnp.sum(ka[hd], axis=0, keepdims=True) * (1.0 / ts)

    qb = proj(_C_QB)
    qb_ref[0] = (qb * QK_SCALE).astype(BF16)
    store_q_t(qnt_ref, rope(qb) * QK_SCALE)
    kskw = proj(_C_KSKW, 128)
    partner = jnp.where(first128, pltpu.roll(kskw, 96, 1), pltpu.roll(kskw, 32, 1))
    kskw = kskw * cos[:, :128] + partner * sin[:, :128]
    slc_hot = jnp.where(lane128 - 64 == (pos128 >> B_SLC_SHIFT), 1.0, 0.0)
    ksa_ref[0] = jnp.where(lo128, kskw, slc_hot).astype(BF16)
    kwp_ref[0] = jnp.where(lo128, pltpu.roll(kskw, 64, 1), 0.0).astype(BF16)
    vsvw_t = jnp.transpose(proj(_C_VSVW, 128))
    vst_ref[0, 0] = vsvw_t[0:HEAD_DIM].astype(BF16)
    vwt_ref[0, 0] = vsvw_t[HEAD_DIM:2 * HEAD_DIM].astype(BF16)
    sub_ref[0] = proj(_C_KCVC, 128)
    for t in range(B_CMP_STRIDE):
        kcvc_ref[0, :, t * 128:(t + 1) * 128] = sub_ref[0, pl.ds(t, ts // B_CMP_STRIDE, stride=B_CMP_STRIDE), :]

    qc_ref[0] = (proj(_C_QC) * QK_SCALE).astype(BF16)
    kc_ref[0] = proj(_C_KC).astype(BF16)
    vc_ref[0] = proj(_C_VC).astype(BF16)

    def store_sub(ref, v, dil):
        if dil == 1:
            ref[0, 0] = v.astype(BF16)
            return
        n = ts // dil
        for half in range(2):
            sub_ref[half] = v[:, half * 128:(half + 1) * 128]
        for c in range(dil):
            for half in range(2):
                ref[0, c, :, half * 128:(half + 1) * 128] = (
                    sub_ref[half, pl.ds(c, n, stride=dil), :].astype(BF16))

    groups = ((qd0_ref, kd0_ref, vd0_ref), (qd1_ref, kd1_ref, vd1_ref), (qd2_ref, kd2_ref, vd2_ref))
    for g, (q_ref, k_ref, v_ref) in enumerate(groups):
        dil = D_PATTERNS[g][1]
        store_sub(q_ref, rope(proj(_C_QD + 256 * g)) * QK_SCALE, dil)
        store_sub(k_ref, rope(proj(_C_KD + 256 * g)), dil)
        store_sub(v_ref, proj(_C_VD + 256 * g), dil)


def _inproj(x, norm_w, w1, cos_t, sin_t):
    b, s, d = x.shape
    ts = TOK_TILE
    nt = s // ts
    tok = lambda w, dt: jax.ShapeDtypeStruct((b, s, w), dt)
    tok_spec = lambda w: pl.BlockSpec((1, ts, w), lambda i, j: (i, j, 0))
    head = jax.ShapeDtypeStruct((b, N_HEADS, s, 128), BF16)
    head_spec = pl.BlockSpec((1, N_HEADS, ts, 128), lambda i, j: (i, 0, j, 0))
    q_t = jax.ShapeDtypeStruct((b, N_HEADS, LANES, s), BF16)
    q_t_spec = pl.BlockSpec((1, N_HEADS, LANES, ts), lambda i, j: (i, 0, 0, j))
    v_t = jax.ShapeDtypeStruct((b, nt, HEAD_DIM, ts), BF16)
    v_t_spec = pl.BlockSpec((1, 1, HEAD_DIM, ts), lambda i, j: (i, j, 0, 0))
    nrow = ts // B_CMP_STRIDE
    out_shape = [q_t, head, jax.ShapeDtypeStruct((b, N_HEADS, nt, HEAD_DIM, ts), BF16),
                 jax.ShapeDtypeStruct((b, nt, N_HEADS, 128), F32),
                 tok(256, BF16), q_t, tok(128, BF16), tok(128, BF16), v_t, v_t,
                 jax.ShapeDtypeStruct((b, s // B_CMP_STRIDE, B_CMP_STRIDE * 128), F32),
                 tok(256, BF16), tok(256, BF16), tok(256, BF16)]
    out_specs = [q_t_spec, head_spec,
                 pl.BlockSpec((1, N_HEADS, 1, HEAD_DIM, ts), lambda i, j: (i, 0, j, 0, 0)),
                 pl.BlockSpec((1, 1, N_HEADS, 128), lambda i, j: (i, j, 0, 0)),
                 tok_spec(256), q_t_spec, tok_spec(128), tok_spec(128), v_t_spec, v_t_spec,
                 pl.BlockSpec((1, nrow, B_CMP_STRIDE * 128), lambda i, j: (i, j, 0)),
                 tok_spec(256), tok_spec(256), tok_spec(256)]
    for _, dil in D_PATTERNS:
        out_shape += [jax.ShapeDtypeStruct((b, dil, s // dil, 256), BF16)] * 3
        out_specs += [pl.BlockSpec((1, dil, ts // dil, 256), lambda i, j: (i, 0, j, 0))] * 3
    return pl.pallas_call(
        _inproj_kernel,
        grid=(b, nt),
        in_specs=[
            pl.BlockSpec((1, ts, d), lambda i, j: (i, j, 0)),
            pl.BlockSpec((1, d), lambda i, j: (0, 0)),
            pl.BlockSpec((d, _W1_COLS), lambda i, j: (0, 0)),
            pl.BlockSpec((ts, 256), lambda i, j: (j, 0)),
            pl.BlockSpec((ts, 256), lambda i, j: (j, 0)),
        ],
        out_specs=out_specs,
        out_shape=out_shape,
        scratch_shapes=[pltpu.VMEM((2, ts, 128), F32)],
        compiler_params=_params("parallel", "arbitrary"),
        name="inproj",
    )(x, norm_w, w1, cos_t, sin_t)


def _moba_kernel(qt_ref, k_ref, vt_ref, km_ref, o_ref, qc_ref, *two_pass_refs, nb, topk):
    tq = qt_ref.shape[3]
    tk = tq
    qb = pl.program_id(1)
    blk_t = lax.broadcasted_iota(jnp.int32, (nb, tq), 0)
    past_t = blk_t < qb
    key = lax.broadcasted_iota(jnp.int32, (tk, tq), 0)
    qry = lax.broadcasted_iota(jnp.int32, (tk, tq), 1)
    causal = key <= qry

    for h in range(N_HEADS):
        q_t = qt_ref[0, h]
        gate = jnp.where(past_t, _dot(km_ref[0, h].astype(BF16), q_t), -jnp.inf)
        rank = jnp.zeros((nb, tq), jnp.int32)
        for i in range(nb):
            row = gate[i:i + 1, :]
            beats = (row > gate) | ((row == gate) & (blk_t > i))
            rank = rank + jnp.where(beats, 1, 0)
        keep = ((rank < topk) & past_t) | (blk_t == qb)
        qc_ref[h] = _with_mask_rows(q_t, jnp.where(keep, 0.0, MASKED))

    def scores_of(j):
        off = pl.multiple_of(j * tk, tk)
        return [_dot(k_ref[0, h, pl.ds(off, tk), :], qc_ref[h]) for h in range(N_HEADS)]

    def values_of(j):
        return [vt_ref[0, h, j] for h in range(N_HEADS)]

    o_ref[0] = _heads_to_tokens(_attend_two_pass(qb, scores_of, values_of, causal, *two_pass_refs))


def _moba(qat, ka, vat, kmean):
    b, _, s, _ = ka.shape
    tq = A_BLOCK
    nb = s // A_BLOCK
    kern = functools.partial(_moba_kernel, nb=nb, topk=min(A_TOPK, nb))
    return pl.pallas_call(
        kern,
        grid=(b, nb),
        in_specs=[pl.BlockSpec((1, N_HEADS, LANES, tq), lambda i, j: (i, 0, 0, j)),
                  pl.BlockSpec((1, N_HEADS, s, 128), lambda i, j: (i, 0, 0, 0)),
                  pl.BlockSpec((1, N_HEADS, nb, HEAD_DIM, tq), lambda i, j: (i, 0, 0, 0, 0)),
                  pl.BlockSpec((1, N_HEADS, nb, 128), lambda i, j: (i, 0, 0, 0))],
        out_specs=pl.BlockSpec((1, tq, 256), lambda i, j: (i, j, 0)),
        out_shape=jax.ShapeDtypeStruct((b, s, 256), F32),
        scratch_shapes=[pltpu.VMEM((N_HEADS, LANES, tq), BF16)] + _two_pass_scratch(nb, tq),
        compiler_params=_params("parallel", "arbitrary"),
        name="moba",
    )(qat, ka, vat, kmean)


def _gelu_tanh(x):
    return 0.5 * x * (1.0 + jnp.tanh(np.sqrt(2.0 / np.pi).astype(np.float32) * (x + 0.044715 * (x * x * x))))


def _cmp_kernel(xr_ref, pelo_ref, pehi_ref, wlo_ref, whi_ref, w2k_ref, w2v_ref, q_ref, ovt_ref,
                o_ref, qbias_ref, kbig_ref, vbig_ref, *, nsel, topn):
    tq = q_ref.shape[1]
    nr = xr_ref.shape[1]
    nc = nr - 1
    qi = pl.program_id(1)

    @pl.when(qi == 0)
    def _():
        xr = xr_ref[0]
        lo = (xr + pelo_ref[...]).astype(BF16)
        hi = (pltpu.roll(xr, nr - 1, 0) + pehi_ref[...]).astype(BF16)
        hid = _dot(lo, wlo_ref[...]) + _dot(hi, whi_ref[...])
        act = _gelu_tanh(hid).astype(BF16)
        kbig_ref[...] = _dot(act, w2k_ref[...]).astype(BF16)
        vbig_ref[...] = _dot(act, w2v_ref[...]).astype(BF16)

    q = q_ref[0]
    head = _head_mask((tq, 256))
    tpos = qi * tq + lax.broadcasted_iota(jnp.int32, (tq, nr), 0)
    n = lax.broadcasted_iota(jnp.int32, (tq, nr), 1)
    vis = (n * B_CMP_STRIDE + (B_CMP_LEN - 1) <= tpos) & (n < nc)
    ovt = ovt_ref[...]
    out = jnp.zeros((tq, 256), F32)
    imp = jnp.zeros((nsel, tq), F32)
    for h in range(N_HEADS):
        qh = _keep_head(q, head, h)
        s = jnp.where(vis, _dot_nt(qh, kbig_ref[:, h * 256:(h + 1) * 256]), MASKED)
        mx = jnp.max(s, axis=1, keepdims=True)
        e = jnp.where(vis, jnp.exp(s - mx), 0.0)
        den = jnp.maximum(jnp.sum(e, axis=1, keepdims=True), 1e-30)
        p = (e / den).astype(BF16)
        out = out + _dot(p, vbig_ref[:, h * 256:(h + 1) * 256])
        imp = imp + _dot_nt(ovt, p)
    o_ref[0] = out

    jj = lax.broadcasted_iota(jnp.int32, (nsel, tq), 0)
    cur = (qi * tq + lax.broadcasted_iota(jnp.int32, (nsel, tq), 1)) >> B_SLC_SHIFT
    forced = (jj == 0) | (jj == cur) | (jj == cur - 1)
    valid = jj <= cur
    imp = jnp.where(valid, jnp.where(forced, jnp.inf, imp), -jnp.inf)
    rank = jnp.zeros((nsel, tq), jnp.int32)
    for i in range(nsel):
        row = imp[i:i + 1, :]
        beats = (row > imp) | ((row == imp) & (jj > i))
        rank = rank + jnp.where(beats, 1, 0)
    qbias_ref[0] = jnp.where((rank < topn) & valid, 0.0, MASKED).astype(BF16)


def _nsa_cmp(xr, qb, pe_lo, pe_hi, w_lo, w_hi, w2k, w2v, ovt):
    b, s, _ = qb.shape
    nr = s // B_CMP_STRIDE
    nsel = s // B_SLC_LEN
    assert nsel <= LANES - HEAD_DIM, "the selection mask lives in the spare contraction rows of q"
    tq = TOK_TILE
    kern = functools.partial(_cmp_kernel, nsel=nsel, topn=min(B_SLC_TOPN, nsel))
    full = lambda a: pl.BlockSpec(a.shape, lambda i, j: (0,) * a.ndim)
    return pl.pallas_call(
        kern,
        grid=(b, s // tq),
        in_specs=[pl.BlockSpec((1, nr, B_CMP_STRIDE * 128), lambda i, j: (i, 0, 0)),
                  full(pe_lo), full(pe_hi), full(w_lo), full(w_hi), full(w2k), full(w2v),
                  pl.BlockSpec((1, tq, 256), lambda i, j: (i, j, 0)), full(ovt)],
        out_specs=[pl.BlockSpec((1, tq, 256), lambda i, j: (i, j, 0)),
                   pl.BlockSpec((1, nsel, tq), lambda i, j: (i, 0, j))],
        out_shape=[jax.ShapeDtypeStruct((b, s, 256), F32),
                   jax.ShapeDtypeStruct((b, nsel, s), BF16)],
        scratch_shapes=[pltpu.VMEM((nr, 1024), BF16), pltpu.VMEM((nr, 1024), BF16)],
        compiler_params=_params("parallel", "arbitrary"),
        name="nsa_cmp",
    )(xr, pe_lo, pe_hi, w_lo, w_hi, w2k, w2v, qb, ovt)


def _nsa_kernel(qt_ref, qbias_ref, ksa_ref, kwp_ref, vst_ref, vwt_ref, oslc_ref, owin_ref,
                qc_ref, *two_pass_refs, ntile):
    tq = qt_ref.shape[3]
    tk = tq
    qi = pl.program_id(1)
    q_t = [qt_ref[0, h] for h in range(N_HEADS)]
    key = lax.broadcasted_iota(jnp.int32, (tk, tq), 0)
    qry = lax.broadcasted_iota(jnp.int32, (tk, tq), 1)

    bias_t = qbias_ref[0]
    for h in range(N_HEADS):
        qc_ref[h] = _with_mask_rows(q_t[h], bias_t)

    def scores_of(j):
        kt = ksa_ref[0, pl.ds(pl.multiple_of(j * tk, tk), tk), :]
        return [_dot(kt, qc_ref[h]) for h in range(N_HEADS)]

    def values_of(j):
        return [vst_ref[0, j]] * N_HEADS

    oslc_ref[0] = _heads_to_tokens(
        _attend_two_pass(qi, scores_of, values_of, key <= qry, *two_pass_refs))

    t0 = jnp.maximum(qi + 1 - ntile, 0)
    kt = kwp_ref[0, pl.ds(pl.multiple_of(t0 * tk, tk), ntile * tk), :]
    span = ntile * tk
    dist = (qi * tq + lax.broadcasted_iota(jnp.int32, (span, tq), 1)) - (
        t0 * tk + lax.broadcasted_iota(jnp.int32, (span, tq), 0))
    keep = (dist >= 0) & (dist < B_WINDOW)
    scores = [jnp.where(keep, _dot(kt, q_t[h]), MASKED) for h in range(N_HEADS)]
    outs = []
    for s in scores:
        m = jnp.max(s, axis=0, keepdims=True)
        p = jnp.exp(s - m)
        l = jnp.sum(p, axis=0, keepdims=True)
        p = p.astype(BF16)
        o = _dot(vwt_ref[0, t0], p[0:tk])
        for i in range(1, ntile):
            o = o + _dot(vwt_ref[0, t0 + i], p[i * tk:(i + 1) * tk])
        outs.append(o / l)
    owin_ref[0] = _heads_to_tokens(outs)


def _nsa_slc_win(qnt, qbias, ksa, kwp, vst, vwt):
    b, s, _ = ksa.shape
    tq = TOK_TILE
    nt = s // tq
    nsel = s // B_SLC_LEN
    ntile = min(B_WINDOW // tq + 1, nt)
    seq = pl.BlockSpec((1, s, 128), lambda i, j: (i, 0, 0))
    v_t = pl.BlockSpec((1, nt, HEAD_DIM, tq), lambda i, j: (i, 0, 0, 0))
    ospec = pl.BlockSpec((1, tq, 256), lambda i, j: (i, j, 0))
    return pl.pallas_call(
        functools.partial(_nsa_kernel, ntile=ntile),
        grid=(b, nt),
        in_specs=[pl.BlockSpec((1, N_HEADS, LANES, tq), lambda i, j: (i, 0, 0, j)),
                  pl.BlockSpec((1, nsel, tq), lambda i, j: (i, 0, j)), seq, seq, v_t, v_t],
        out_specs=[ospec, ospec],
        out_shape=[jax.ShapeDtypeStruct((b, s, 256), F32)] * 2,
        scratch_shapes=[pltpu.VMEM((N_HEADS, LANES, tq), BF16)] + _two_pass_scratch(nt, tq),
        compiler_params=_params("parallel", "arbitrary"),
        name="nsa_slc_win",
    )(qnt, qbias, ksa, kwp, vst, vwt)


def _sb_kernel(q_ref, k_ref, v_ref, o_ref, c_ref, acc_ref):
    tq = q_ref.shape[1]
    tk = tq
    qi = pl.program_id(1)
    q = q_ref[0]
    head = _head_mask((tq, 256))
    qh = [_keep_head(q, head, h) for h in range(N_HEADS)]
    rows = lax.broadcasted_iota(jnp.int32, (tq, tk), 0)
    cols = lax.broadcasted_iota(jnp.int32, (tq, tk), 1)
    after = jnp.where(rows > cols, 1.0, 0.0).astype(BF16)
    after2 = jnp.concatenate([after, after], axis=0)
    c_ref[...] = jnp.zeros(c_ref.shape, F32)
    acc_ref[...] = jnp.zeros(acc_ref.shape, F32)

    def tile(j, strict_causal):
        off = pl.multiple_of(j * tk, tk)
        kt = k_ref[0, pl.ds(off, tk), :]
        vt = v_ref[0, pl.ds(off, tk), :]
        zs = [_dot_nt(qh[h], kt) for h in range(N_HEADS)]
        log_betas, log_1ms, tails = [], [], []
        for z in zs:
            log_beta = jnp.minimum(z, 0.0) - jnp.log(1.0 + jnp.exp(-jnp.abs(z)))
            log_1m = log_beta - z
            if strict_causal is not None:
                log_1m = jnp.where(strict_causal, log_1m, 0.0)
            hi = log_1m.astype(BF16)
            lo = (log_1m - hi.astype(F32)).astype(BF16)
            tails.append(_dot(jnp.concatenate([hi, lo], axis=1), after2))
            log_betas.append(log_beta)
            log_1ms.append(log_1m)
        alive = None
        for h in range(N_HEADS):
            c = c_ref[h]
            a = jnp.exp(log_betas[h] + (tails[h] + c))
            if strict_causal is not None:
                a = jnp.where(strict_causal, a, 0.0)
            acc_ref[h] += _dot(a.astype(BF16), vt)
            c_new = c + jnp.sum(log_1ms[h], axis=1, keepdims=True)
            c_ref[h] = c_new
            alive = c_new if alive is None else jnp.maximum(alive, c_new)
        return jnp.max(alive)

    def cond(carry):
        j, cmax = carry
        return (j >= 1) & (cmax > SB_DEAD)

    def body(carry):
        j, _ = carry
        tile(j, None)
        return j - 2, tile(j - 1, None)

    j, cmax = lax.while_loop(cond, body, (qi - 1, tile(qi, cols < rows)))

    @pl.when((j == 0) & (cmax > SB_DEAD))
    def _():
        tile(0, None)
    out = acc_ref[0]
    for h in range(1, N_HEADS):
        out = jnp.where(head == h, acc_ref[h], out)
    o_ref[0] = out


def _stickbreak(qc, kc, vc):
    b, s, _ = qc.shape
    tq = DENSE_TQ
    seq = pl.BlockSpec((1, s, 256), lambda i, j: (i, 0, 0))
    tile = pl.BlockSpec((1, tq, 256), lambda i, j: (i, j, 0))
    return pl.pallas_call(
        _sb_kernel,
        grid=(b, s // tq),
        in_specs=[tile, seq, seq],
        out_specs=tile,
        out_shape=jax.ShapeDtypeStruct((b, s, 256), F32),
        scratch_shapes=[pltpu.VMEM((N_HEADS, tq, LANES), F32), pltpu.VMEM((N_HEADS, tq, 256), F32)],
        compiler_params=_params("parallel", "arbitrary"),
        name="stickbreak",
    )(qc, kc, vc)


def _dil_kernel(q_ref, k_ref, v_ref, o_ref, lse_ref, *, window, span, tq):
    nq = q_ref.shape[1] // tq
    head = _head_mask((tq, 256))
    rows = lax.broadcasted_iota(jnp.int32, (tq, span), 0)
    cols = lax.broadcasted_iota(jnp.int32, (tq, span), 1)
    qs, ks, vs, keeps = [], [], [], []
    for t in range(nq):
        q0 = (pl.program_id(1) * nq + t) * tq
        k0 = pl.multiple_of(jnp.maximum(q0 + tq - span, 0), tq)
        q = q_ref[0, t * tq:(t + 1) * tq, :]
        dist = (q0 + rows) - (k0 + cols)
        qs += [_keep_head(q, head, h) for h in range(N_HEADS)]
        ks += [k_ref[0, pl.ds(k0, span), :]] * N_HEADS
        vs += [v_ref[0, pl.ds(k0, span), :]] * N_HEADS
        keeps += [(dist >= 0) & (dist <= window)] * N_HEADS
    o, m, l = _softmax_once(qs, ks, vs, keeps)
    for t in range(nq):
        out = o[t * N_HEADS]
        lse = jnp.broadcast_to(m[t * N_HEADS] + jnp.log(l[t * N_HEADS]), (tq, 256))
        for h in range(1, N_HEADS):
            i = t * N_HEADS + h
            out = jnp.where(head == h, o[i], out)
            lse = jnp.where(head == h, m[i] + jnp.log(l[i]), lse)
        o_ref[0, t * tq:(t + 1) * tq, :] = out
        lse_ref[0, t * tq:(t + 1) * tq, :] = lse


def _dilated_group(q, k, v, window):
    n, L, _ = q.shape
    tq = DENSE_TQ
    assert window == tq
    nq = next(n for n in (4, 2, 1) if L % (n * tq) == 0)
    kern = functools.partial(_dil_kernel, window=window, span=min(2 * tq, L), tq=tq)
    seq = pl.BlockSpec((1, L, 256), lambda i, j: (i, 0, 0))
    tile = pl.BlockSpec((1, nq * tq, 256), lambda i, j: (i, j, 0))
    return pl.pallas_call(
        kern,
        grid=(n, L // (nq * tq)),
        in_specs=[tile, seq, seq],
        out_specs=[tile, tile],
        out_shape=[jax.ShapeDtypeStruct((n, L, 256), F32)] * 2,
        compiler_params=_params("parallel", "arbitrary"),
        name="dilated",
    )(q, k, v)


_C_GATE, _C_MERGE, _C_NSAG = 0, 1024, 5120
_W2_COLS = 5248


def _out_kernel(x_ref, nw_ref, w2_ref, oa_ref, ocmp_ref, oslc_ref, owin_ref, oc_ref,
                od0_ref, od1_ref, od2_ref, ls0_ref, ls1_ref, ls2_ref,
                wup_ref, wout_ref, fnw_ref, spread_ref, y_ref, acc_ref, sub_ref, *, final):
    ts = x_ref.shape[1]
    x = x_ref[0]
    h = _rmsnorm(x, nw_ref[...]).astype(BF16)

    def proj(c0, w):
        return _dot(h, w2_ref[:, c0:c0 + w])

    def token_order(ref, dil, slot):
        if dil == 1:
            return ref[0, 0]
        n = ts // dil
        for c in range(dil):
            for half in range(2):
                sub_ref[slot, half, pl.ds(c, n, stride=dil), :] = ref[0, c, :, half * 128:(half + 1) * 128]
        return jnp.concatenate([sub_ref[slot, 0], sub_ref[slot, 1]], axis=1)

    gate = jax.nn.sigmoid(proj(_C_NSAG, 128))
    hi = gate.astype(BF16)
    rest = gate - hi.astype(F32)
    mid = rest.astype(BF16)
    lo = (rest - mid.astype(F32)).astype(BF16)
    spread = spread_ref[...]
    ng = _dot(hi, spread) + _dot(mid, spread) + _dot(lo, spread)
    o_b = ng[:, 0:256] * ocmp_ref[0] + ng[:, 256:512] * oslc_ref[0] + ng[:, 512:768] * owin_ref[0]

    dils = [dil for _, dil in D_PATTERNS]
    od = [token_order(r, d, i) for i, (r, d) in enumerate(zip((od0_ref, od1_ref, od2_ref), dils))]
    ls = [token_order(r, d, 3 + i) for i, (r, d) in enumerate(zip((ls0_ref, ls1_ref, ls2_ref), dils))]
    mx = jnp.maximum(jnp.maximum(ls[0], ls[1]), ls[2])
    e = [jnp.exp(l - mx) for l in ls]
    den = e[0] + e[1] + e[2]
    o_d = (e[0] / den) * od[0] + (e[1] / den) * od[1] + (e[2] / den) * od[2]

    branches = (oa_ref[0], o_b, oc_ref[0], o_d)
    acc_ref[...] = jnp.zeros(acc_ref.shape, F32)
    for i in range(N_BRANCH):
        g = proj(_C_GATE + 256 * i, 256)
        wide = (branches[i] * (g * jax.nn.sigmoid(g))).astype(BF16)
        for half in range(2):
            c0 = 512 * half
            u = _dot(wide, wup_ref[i, :, c0:c0 + 512])
            mg = jax.nn.sigmoid(proj(_C_MERGE + 1024 * i + c0, 512))
            acc_ref[:, c0:c0 + 512] += mg * u
    out = x + _dot(acc_ref[...].astype(BF16), wout_ref[...])
    if final:
        out = _rmsnorm(out, fnw_ref[...])
    y_ref[0] = out


def _merge_out(x, norm_w, w2, token_outs, sub_outs, w_up, w_out, final_norm_w, final):
    b, s, d = x.shape
    ts = TOK_TILE
    tile256 = pl.BlockSpec((1, ts, 256), lambda i, j: (i, j, 0))
    sub_specs = [pl.BlockSpec((1, dil, ts // dil, 256), lambda i, j: (i, 0, j, 0)) for _, dil in D_PATTERNS]
    const = lambda a: pl.BlockSpec(a.shape, lambda i, j: (0,) * a.ndim)
    kern = functools.partial(_out_kernel, final=final)
    gate_id = np.arange(3 * BRANCH_W) // HEAD_DIM
    spread = jnp.asarray(np.arange(128)[:, None] == gate_id[None, :], dtype=BF16)
    return pl.pallas_call(
        kern,
        grid=(b, s // ts),
        in_specs=[pl.BlockSpec((1, ts, d), lambda i, j: (i, j, 0)), const(norm_w), const(w2)]
        + [tile256] * 5 + sub_specs * 2
        + [const(w_up), const(w_out), const(final_norm_w), const(spread)],
        out_specs=pl.BlockSpec((1, ts, d), lambda i, j: (i, j, 0)),
        out_shape=jax.ShapeDtypeStruct((b, s, d), F32),
        scratch_shapes=[pltpu.VMEM((ts, d), F32), pltpu.VMEM((6, 2, ts, 128), F32)],
        compiler_params=_params("parallel", "arbitrary"),
        name="merge_out",
    )(x, norm_w, w2, *token_outs, *sub_outs, w_up, w_out, final_norm_w, spread)


def _rope_tables(s):
    half = HEAD_DIM // 2
    inv = ROPE_THETA ** (-jnp.arange(half, dtype=F32) / half)
    ang = jnp.arange(s).astype(F32)[:, None] * inv[None, :]
    cos, sin = jnp.cos(ang), jnp.sin(ang)
    cos_t = jnp.tile(jnp.concatenate([cos, cos], axis=-1), (1, N_HEADS))
    sin_t = jnp.tile(jnp.concatenate([-sin, sin], axis=-1), (1, N_HEADS))
    return cos_t, sin_t


def _layer_weights(w_in, cmp_pos, cmp_w1, cmp_w2):
    o = np.concatenate([[0], np.cumsum(IN_SIZES)])
    qa, ka, va, ga, qb, kvb, gb, nsag, qkvc, gc, qkvd, gd, merge = [int(v) for v in o[:-1]]
    cols = lambda a, w: w_in[:, a:a + w]
    hd = HEAD_DIM
    w1 = jnp.concatenate([
        cols(qa, 256), cols(ka, 256), cols(va, 256), cols(qb, 256),
        cols(kvb + 2 * hd, hd), cols(kvb + 4 * hd, hd),
        cols(kvb, 2 * hd),
        cols(kvb + 3 * hd, hd), cols(kvb + 5 * hd, hd),
        cols(qkvc, 768), cols(qkvd, 2304)], axis=1)
    gate_cols = jnp.concatenate([cols(ga, 256), cols(gb, 256), cols(gc, 256), cols(gd, 256)], axis=1)
    nsag_cols = jnp.pad(cols(nsag, 3 * N_HEADS), ((0, 0), (0, 128 - 3 * N_HEADS)))
    w2 = jnp.concatenate([gate_cols, cols(merge, 4096), nsag_cols], axis=1)

    half = B_CMP_STRIDE
    pe = jnp.concatenate([cmp_pos[0], cmp_pos[1]], axis=-1)
    pe_lo = pe[:half].reshape(1, half * 128)
    pe_hi = pe[half:].reshape(1, half * 128)
    w1k = cmp_w1[0].reshape(B_CMP_LEN, hd, B_CMP_HIDDEN)
    w1v = cmp_w1[1].reshape(B_CMP_LEN, hd, B_CMP_HIDDEN)
    zw = jnp.zeros_like(w1k)
    w1_big = jnp.concatenate([jnp.concatenate([w1k, zw], axis=2),
                              jnp.concatenate([zw, w1v], axis=2)], axis=1)
    w_lo = w1_big[:half].reshape(half * 128, 2 * B_CMP_HIDDEN).astype(BF16)
    w_hi = w1_big[half:].reshape(half * 128, 2 * B_CMP_HIDDEN).astype(BF16)

    def placed(w2h, top):
        blocks = []
        for h in range(N_HEADS):
            blk = jnp.zeros((2 * B_CMP_HIDDEN, 256), F32)
            r0 = 0 if top else B_CMP_HIDDEN
            blk = blk.at[r0:r0 + B_CMP_HIDDEN, h * hd:(h + 1) * hd].set(w2h)
            blocks.append(blk)
        return jnp.concatenate(blocks, axis=1).astype(BF16)

    return w1, w2, pe_lo, pe_hi, w_lo, w_hi, placed(cmp_w2[0], True), placed(cmp_w2[1], False)


def _overlap_t(s):
    nr = s // B_CMP_STRIDE
    nc = nr - 1
    nsel = s // B_SLC_LEN
    starts = np.arange(nr) * B_CMP_STRIDE
    j = np.arange(nsel)
    ov = ((starts[None, :] < (j[:, None] + 1) * B_SLC_LEN)
          & (starts[None, :] + B_CMP_LEN > j[:, None] * B_SLC_LEN)
          & (np.arange(nr)[None, :] < nc))
    return jnp.asarray(ov.astype(np.float32)).astype(BF16)


def _layer(x, norm_w, w_in, cmp_pos, cmp_w1, cmp_w2, w_up, w_out, final_norm_w, final, tables, ovt):
    b, s, _ = x.shape
    w1, w2, pe_lo, pe_hi, w_lo, w_hi, w2k, w2v = _layer_weights(w_in, cmp_pos, cmp_w1, cmp_w2)
    nw = norm_w.reshape(1, -1)
    (qat, ka, vat, kmean, qb, qnt, ksa, kwp, vst, vwt, kcvc, qc, kc, vc, *qkvd) = _inproj(x, nw, w1, *tables)

    o_a = _moba(qat, ka, vat, kmean.transpose(0, 2, 1, 3))
    o_cmp, qbias = _nsa_cmp(kcvc, qb, pe_lo, pe_hi, w_lo, w_hi, w2k, w2v, ovt)
    o_slc, o_win = _nsa_slc_win(qnt, qbias, ksa, kwp, vst, vwt)
    o_c = _stickbreak(qc, kc, vc)

    od, ls = [], []
    for g, (window, dil) in enumerate(D_PATTERNS):
        q, k, v = (a.reshape(b * dil, s // dil, 256) for a in qkvd[3 * g:3 * g + 3])
        o, l = _dilated_group(q, k, v, window // dil)
        od.append(o.reshape(b, dil, s // dil, 256))
        ls.append(l.reshape(b, dil, s // dil, 256))

    return _merge_out(x, nw, w2, (o_a, o_cmp, o_slc, o_win, o_c), (*od, *ls),
                      w_up.astype(BF16), w_out.astype(BF16), final_norm_w.reshape(1, -1), final)


def kernel(x, norm_w, w_in, nsa_cmp_pos, nsa_cmp_w1, nsa_cmp_w2, w_up, w_out, final_norm_w):
    depth = norm_w.shape[0]
    s = x.shape[1]
    tables = _rope_tables(s)
    ovt = _overlap_t(s)
    w_in = w_in.astype(BF16)
    for layer in range(depth):
        x = _layer(x, norm_w[layer], w_in[layer], nsa_cmp_pos[layer], nsa_cmp_w1[layer],
                   nsa_cmp_w2[layer], w_up[layer], w_out[layer], final_norm_w,
                   layer == depth - 1, tables, ovt)
    return x
```

```python
import functools

import numpy as np
import jax
import jax.numpy as jnp
from jax import lax
from jax.experimental import pallas as pl
from jax.experimental.pallas import tpu as pltpu

F32 = jnp.float32
BF16 = jnp.bfloat16

HEAD_DIM = 64
N_HEADS = 4
BRANCH_W = N_HEADS * HEAD_DIM
ROPE_THETA = 10000.0
NORM_EPS = 1e-6
QK_SCALE = HEAD_DIM ** -0.5

A_BLOCK = 256
A_SHIFT = 8
A_TOPK = 3
B_CMP_LEN = 32
B_CMP_STRIDE = 16
B_CMP_HIDDEN = 256
B_SLC_LEN = 64
B_SLC_SHIFT = 6
B_SLC_TOPN = 16
B_WINDOW = 512
D_PATTERNS = ((128, 1), (512, 4), (2048, 16))
N_BRANCH = 4

IN_SIZES = (
    BRANCH_W, BRANCH_W, BRANCH_W, BRANCH_W,
    BRANCH_W, 6 * HEAD_DIM, BRANCH_W, 3 * N_HEADS,
    3 * BRANCH_W, BRANCH_W,
    3 * len(D_PATTERNS) * N_HEADS * HEAD_DIM, BRANCH_W,
    N_BRANCH * 1024,
)

MASKED = -2e30
LOG2_E = 1.4426950408889634
SB_DEAD = -120.0
VMEM_LIMIT = 56 * 1024 * 1024
LANES = 128

TOK_TILE = 256
DENSE_TQ = 128


def _dot(a, b):
    return jnp.dot(a, b, preferred_element_type=F32)


def _dot_nt(a, b):
    return lax.dot_general(a, b, (((1,), (1,)), ((), ())), preferred_element_type=F32)


def _params(*sem):
    return pltpu.CompilerParams(dimension_semantics=sem, vmem_limit_bytes=VMEM_LIMIT)


def _rmsnorm(x, w):
    y = x * lax.rsqrt(jnp.mean(x * x, axis=-1, keepdims=True) + NORM_EPS)
    return y * w


def _head_mask(shape):
    return lax.broadcasted_iota(jnp.int32, shape, 1) >> 6


def _keep_head(q, head, h):
    return jnp.where(head == h, q.astype(F32), 0.0).astype(BF16)


def _with_mask_rows(q_t, bias_t):
    n, tq = bias_t.shape
    parts = [q_t[:HEAD_DIM], bias_t.astype(BF16)]
    if HEAD_DIM + n < LANES:
        parts.append(jnp.zeros((LANES - HEAD_DIM - n, tq), BF16))
    return jnp.concatenate(parts, axis=0)


def _heads_to_tokens(outs_t):
    return jnp.transpose(jnp.concatenate(outs_t, axis=0))


def _two_pass_scratch(n_tiles, tq):
    return [pltpu.VMEM((N_HEADS, n_tiles, tq, tq), F32), pltpu.VMEM((N_HEADS, HEAD_DIM, tq), F32),
            pltpu.VMEM((N_HEADS, 8, tq), F32), pltpu.VMEM((N_HEADS, 8, tq), F32)]


def _attend_two_pass(n_full, scores_of, values_of, keep_last, s_ref, acc_ref, peak_ref, total_ref,
                     static_tiles=0):
    heads, _, tk, tq = s_ref.shape

    def fold(x, op):
        return op(x.reshape(tk // 8, 8, tq), axis=0)

    def for_tiles(count, fn):
        def quad(i, carry):
            for t in range(4):
                fn(4 * i + t)
            return carry

        lax.fori_loop(0, count // 4, quad, 0)
        done = (count // 4) * 4

        @pl.when((count & 2) != 0)
        def _():
            fn(done)
            fn(done + 1)

        @pl.when((count & 1) != 0)
        def _():
            fn(count - 1)

    def park(j, scores):
        for h, s in enumerate(scores):
            s = s * LOG2_E
            s_ref[h, j] = s
            peak_ref[h] = jnp.maximum(peak_ref[h], fold(s, jnp.max))

    peak_ref[...] = jnp.full(peak_ref.shape, MASKED, F32)
    if static_tiles:
        for j in range(static_tiles):
            park(j, scores_of(j))
    else:
        for_tiles(n_full, lambda j: park(j, scores_of(j)))
        park(n_full, [jnp.where(keep_last, s, MASKED) for s in scores_of(n_full)])
    top = [jnp.max(peak_ref[h], axis=0, keepdims=True) for h in range(heads)]

    acc_ref[...] = jnp.zeros(acc_ref.shape, F32)
    total_ref[...] = jnp.zeros(total_ref.shape, F32)

    def weigh(j):
        for h, v_t in enumerate(values_of(j)):
            p = jnp.exp2(s_ref[h, j] - top[h])
            total_ref[h] += fold(p, jnp.sum)
            acc_ref[h] += _dot(v_t, p.astype(BF16))

    if static_tiles:
        for j in range(static_tiles):
            weigh(j)
    else:
        for_tiles(n_full + 1, weigh)
    return [acc_ref[h] / jnp.sum(total_ref[h], axis=0, keepdims=True) for h in range(heads)]


def _softmax_once(qs, ks, vs, keeps):
    scores = [jnp.where(keep, _dot_nt(q, k), MASKED) for q, k, keep in zip(qs, ks, keeps)]
    outs, maxes, sums = [], [], []
    for s, v in zip(scores, vs):
        m = jnp.max(s, axis=1, keepdims=True)
        p = jnp.exp(s - m)
        l = jnp.sum(p, axis=1, keepdims=True)
        outs.append(_dot(p.astype(BF16), v) / l)
        maxes.append(m)
        sums.append(l)
    return outs, maxes, sums


_C_QA, _C_KA, _C_VA, _C_QB = 0, 256, 512, 768
_C_KSKW, _C_KCVC, _C_VSVW = 1024, 1152, 1280
_C_QC, _C_KC, _C_VC = 1408, 1664, 1920
_C_QD, _C_KD, _C_VD = 2176, 2944, 3712
_W1_COLS = 4480


def _inproj_kernel(x_ref, nw_ref, w_ref, cos_ref, sin_ref,
                   qat_ref, ka_ref, vat_ref, kmean_ref,
                   qbt_ref, qnt_ref, ksa_ref, kwp_ref, vst_ref, vwt_ref, kcvc_ref,
                   qc_ref, kc_ref, vc_ref,
                   qd0_ref, kd0_ref, vd0_ref, qd1_ref, kd1_ref, vd1_ref, qd2_ref, kd2_ref, vd2_ref,
                   sub_ref):
    ts = x_ref.shape[1]
    pos0 = pl.program_id(1) * ts
    h = _rmsnorm(x_ref[0], nw_ref[...]).astype(BF16)
    cos = cos_ref[...]
    sin = sin_ref[...]
    lane = lax.broadcasted_iota(jnp.int32, (ts, 256), 1)
    first = (lane & 63) < 32
    lane128 = lax.broadcasted_iota(jnp.int32, (ts, 128), 1)
    pos128 = pos0 + lax.broadcasted_iota(jnp.int32, (ts, 128), 0)
    first128 = (lane128 & 63) < 32
    lo128 = lane128 < 64

    def proj(c0, w=256):
        return _dot(h, w_ref[:, c0:c0 + w])

    def rope(v):
        partner = jnp.where(first, pltpu.roll(v, 224, 1), pltpu.roll(v, 32, 1))
        return v * cos + partner * sin

    def split_heads(v):
        out = []
        for pair in range(2):
            p = v[:, pair * 128:(pair + 1) * 128]
            out.append(jnp.where(lo128, p, 0.0))
            out.append(jnp.where(lo128, pltpu.roll(p, 64, 1), 0.0))
        return out

    def store_q_t(ref, v):
        v_t = jnp.transpose(v)
        for hd in range(N_HEADS):
            ref[0, hd, 0:HEAD_DIM, :] = v_t[hd * HEAD_DIM:(hd + 1) * HEAD_DIM].astype(BF16)
            ref[0, hd, HEAD_DIM:LANES, :] = jnp.zeros((LANES - HEAD_DIM, ts), BF16)

    store_q_t(qat_ref, rope(proj(_C_QA)) * QK_SCALE)
    ka = split_heads(rope(proj(_C_KA)))
    va_t = jnp.transpose(proj(_C_VA))
    block_hot = jnp.where(lane128 - 64 == (pos128 >> A_SHIFT), 1.0, 0.0)
    for hd in range(N_HEADS):
        ka_ref[0, hd] = (ka[hd] + block_hot).astype(BF16)
        vat_ref[0, hd, 0] = va_t[hd * HEAD_DIM:(hd + 1) * HEAD_DIM].astype(BF16)
        kmean_ref[0, 0, pl.ds(hd, 1), :] = jnp.sum(ka[hd], axis=0, keepdims=True) * (1.0 / ts)

    qb = proj(_C_QB)
    store_q_t(qbt_ref, qb * QK_SCALE)
    store_q_t(qnt_ref, rope(qb) * QK_SCALE)
    kskw = proj(_C_KSKW, 128)
    partner = jnp.where(first128, pltpu.roll(kskw, 96, 1), pltpu.roll(kskw, 32, 1))
    kskw = kskw * cos[:, :128] + partner * sin[:, :128]
    slc_hot = jnp.where(lane128 - 64 == (pos128 >> B_SLC_SHIFT), 1.0, 0.0)
    ksa_ref[0] = jnp.where(lo128, kskw, slc_hot).astype(BF16)
    kwp_ref[0] = jnp.where(lo128, pltpu.roll(kskw, 64, 1), 0.0).astype(BF16)
    vsvw_t = jnp.transpose(proj(_C_VSVW, 128))
    vst_ref[0, 0] = vsvw_t[0:HEAD_DIM].astype(BF16)
    vwt_ref[0, 0] = vsvw_t[HEAD_DIM:2 * HEAD_DIM].astype(BF16)
    sub_ref[0] = proj(_C_KCVC, 128)
    for t in range(B_CMP_STRIDE):
        kcvc_ref[0, :, t * 128:(t + 1) * 128] = sub_ref[0, pl.ds(t, ts // B_CMP_STRIDE, stride=B_CMP_STRIDE), :]

    qc_ref[0] = (proj(_C_QC) * QK_SCALE).astype(BF16)
    kc_ref[0] = proj(_C_KC).astype(BF16)
    vc_ref[0] = proj(_C_VC).astype(BF16)

    def store_sub(ref, v, dil):
        if dil == 1:
            ref[0, 0] = v.astype(BF16)
            return
        n = ts // dil
        for half in range(2):
            sub_ref[half] = v[:, half * 128:(half + 1) * 128]
        for c in range(dil):
            for half in range(2):
                ref[0, c, :, half * 128:(half + 1) * 128] = (
                    sub_ref[half, pl.ds(c, n, stride=dil), :].astype(BF16))

    groups = ((qd0_ref, kd0_ref, vd0_ref), (qd1_ref, kd1_ref, vd1_ref), (qd2_ref, kd2_ref, vd2_ref))
    for g, (q_ref, k_ref, v_ref) in enumerate(groups):
        dil = D_PATTERNS[g][1]
        store_sub(q_ref, rope(proj(_C_QD + 256 * g)) * QK_SCALE, dil)
        store_sub(k_ref, rope(proj(_C_KD + 256 * g)), dil)
        store_sub(v_ref, proj(_C_VD + 256 * g), dil)


def _inproj(x, norm_w, w1, cos_t, sin_t):
    b, s, d = x.shape
    ts = TOK_TILE
    nt = s // ts
    tok = lambda w, dt: jax.ShapeDtypeStruct((b, s, w), dt)
    tok_spec = lambda w: pl.BlockSpec((1, ts, w), lambda i, j: (i, j, 0))
    head = jax.ShapeDtypeStruct((b, N_HEADS, s, 128), BF16)
    head_spec = pl.BlockSpec((1, N_HEADS, ts, 128), lambda i, j: (i, 0, j, 0))
    q_t = jax.ShapeDtypeStruct((b, N_HEADS, LANES, s), BF16)
    q_t_spec = pl.BlockSpec((1, N_HEADS, LANES, ts), lambda i, j: (i, 0, 0, j))
    v_t = jax.ShapeDtypeStruct((b, nt, HEAD_DIM, ts), BF16)
    v_t_spec = pl.BlockSpec((1, 1, HEAD_DIM, ts), lambda i, j: (i, j, 0, 0))
    nrow = ts // B_CMP_STRIDE
    out_shape = [q_t, head, jax.ShapeDtypeStruct((b, N_HEADS, nt, HEAD_DIM, ts), BF16),
                 jax.ShapeDtypeStruct((b, nt, N_HEADS, 128), F32),
                 q_t, q_t, tok(128, BF16), tok(128, BF16), v_t, v_t,
                 jax.ShapeDtypeStruct((b, s // B_CMP_STRIDE, B_CMP_STRIDE * 128), F32),
                 tok(256, BF16), tok(256, BF16), tok(256, BF16)]
    out_specs = [q_t_spec, head_spec,
                 pl.BlockSpec((1, N_HEADS, 1, HEAD_DIM, ts), lambda i, j: (i, 0, j, 0, 0)),
                 pl.BlockSpec((1, 1, N_HEADS, 128), lambda i, j: (i, j, 0, 0)),
                 q_t_spec, q_t_spec, tok_spec(128), tok_spec(128), v_t_spec, v_t_spec,
                 pl.BlockSpec((1, nrow, B_CMP_STRIDE * 128), lambda i, j: (i, j, 0)),
                 tok_spec(256), tok_spec(256), tok_spec(256)]
    for _, dil in D_PATTERNS:
        out_shape += [jax.ShapeDtypeStruct((b, dil, s // dil, 256), BF16)] * 3
        out_specs += [pl.BlockSpec((1, dil, ts // dil, 256), lambda i, j: (i, 0, j, 0))] * 3
    return pl.pallas_call(
        _inproj_kernel,
        grid=(b, nt),
        in_specs=[
            pl.BlockSpec((1, ts, d), lambda i, j: (i, j, 0)),
            pl.BlockSpec((1, d), lambda i, j: (0, 0)),
            pl.BlockSpec((d, _W1_COLS), lambda i, j: (0, 0)),
            pl.BlockSpec((ts, 256), lambda i, j: (j, 0)),
            pl.BlockSpec((ts, 256), lambda i, j: (j, 0)),
        ],
        out_specs=out_specs,
        out_shape=out_shape,
        scratch_shapes=[pltpu.VMEM((2, ts, 128), F32)],
        compiler_params=_params("parallel", "arbitrary"),
        name="inproj",
    )(x, norm_w, w1, cos_t, sin_t)


def _moba_kernel(qt_ref, k_ref, vt_ref, km_ref, o_ref, qc_ref, *two_pass_refs, nb, topk):
    tq = qt_ref.shape[3]
    tk = tq
    qb = pl.program_id(1)
    blk_t = lax.broadcasted_iota(jnp.int32, (nb, tq), 0)
    past_t = blk_t < qb
    key = lax.broadcasted_iota(jnp.int32, (tk, tq), 0)
    qry = lax.broadcasted_iota(jnp.int32, (tk, tq), 1)
    causal = key <= qry

    for h in range(N_HEADS):
        q_t = qt_ref[0, h]
        gate = jnp.where(past_t, _dot(km_ref[0, h].astype(BF16), q_t), -jnp.inf)
        rank = jnp.zeros((nb, tq), jnp.int32)
        for i in range(nb):
            row = gate[i:i + 1, :]
            beats = (row > gate) | ((row == gate) & (blk_t > i))
            rank = rank + jnp.where(beats, 1, 0)
        keep = ((rank < topk) & past_t) | (blk_t == qb)
        qc_ref[h] = _with_mask_rows(q_t, jnp.where(keep, 0.0, MASKED))

    def scores_of(j):
        off = pl.multiple_of(j * tk, tk)
        return [_dot(k_ref[0, h, pl.ds(off, tk), :], qc_ref[h]) for h in range(N_HEADS)]

    def values_of(j):
        return [vt_ref[0, h, j] for h in range(N_HEADS)]

    o_ref[0] = _heads_to_tokens(_attend_two_pass(qb, scores_of, values_of, causal, *two_pass_refs))


def _moba(qat, ka, vat, kmean):
    b, _, s, _ = ka.shape
    tq = A_BLOCK
    nb = s // A_BLOCK
    kern = functools.partial(_moba_kernel, nb=nb, topk=min(A_TOPK, nb))
    return pl.pallas_call(
        kern,
        grid=(b, nb),
        in_specs=[pl.BlockSpec((1, N_HEADS, LANES, tq), lambda i, j: (i, 0, 0, j)),
                  pl.BlockSpec((1, N_HEADS, s, 128), lambda i, j: (i, 0, 0, 0)),
                  pl.BlockSpec((1, N_HEADS, nb, HEAD_DIM, tq), lambda i, j: (i, 0, 0, 0, 0)),
                  pl.BlockSpec((1, N_HEADS, nb, 128), lambda i, j: (i, 0, 0, 0))],
        out_specs=pl.BlockSpec((1, tq, 256), lambda i, j: (i, j, 0)),
        out_shape=jax.ShapeDtypeStruct((b, s, 256), F32),
        scratch_shapes=[pltpu.VMEM((N_HEADS, LANES, tq), BF16)] + _two_pass_scratch(nb, tq),
        compiler_params=_params("parallel", "arbitrary"),
        name="moba",
    )(qat, ka, vat, kmean)


def _gelu_tanh(x):
    return 0.5 * x * (1.0 + jnp.tanh(np.sqrt(2.0 / np.pi).astype(np.float32) * (x + 0.044715 * (x * x * x))))


def _cmp_kernel(xr_ref, pelo_ref, pehi_ref, wlo_ref, whi_ref, w2_ref, qt_ref, ovt_ref,
                o_ref, qbias_ref, kc_ref, vct_ref, *, nsel, topn):
    tq = qt_ref.shape[3]
    nr = xr_ref.shape[1]
    nc = nr - 1
    qi = pl.program_id(1)

    @pl.when(qi == 0)
    def _():
        xr = xr_ref[0]
        lo = (xr + pelo_ref[...]).astype(BF16)
        hi = (pltpu.roll(xr, nr - 1, 0) + pehi_ref[...]).astype(BF16)
        hid = _dot(lo, wlo_ref[...]) + _dot(hi, whi_ref[...])
        act = _gelu_tanh(hid).astype(BF16)
        kv = _dot(act, w2_ref[...])
        kc_ref[...] = kv[:, 0:LANES].astype(BF16)
        vct_ref[...] = jnp.transpose(kv[:, LANES:2 * LANES])[0:HEAD_DIM].astype(BF16)

    tpos = qi * tq + lax.broadcasted_iota(jnp.int32, (nr, tq), 1)
    n = lax.broadcasted_iota(jnp.int32, (nr, tq), 0)
    vis = (n * B_CMP_STRIDE + (B_CMP_LEN - 1) <= tpos) & (n < nc)
    ovt = ovt_ref[...]
    kc = kc_ref[...]
    vct = vct_ref[...]
    scores = [jnp.where(vis, _dot(kc, qt_ref[0, h]), MASKED) for h in range(N_HEADS)]
    outs = []
    imp = jnp.zeros((nsel, tq), F32)
    for s in scores:
        mx = jnp.max(s, axis=0, keepdims=True)
        e = jnp.where(vis, jnp.exp(s - mx), 0.0)
        den = jnp.maximum(jnp.sum(e, axis=0, keepdims=True), 1e-30)
        p = (e / den).astype(BF16)
        outs.append(_dot(vct, p))
        imp = imp + _dot(ovt, p)
    o_ref[0] = _heads_to_tokens(outs)

    jj = lax.broadcasted_iota(jnp.int32, (nsel, tq), 0)
    cur = (qi * tq + lax.broadcasted_iota(jnp.int32, (nsel, tq), 1)) >> B_SLC_SHIFT
    forced = (jj == 0) | (jj == cur) | (jj == cur - 1)
    valid = jj <= cur
    imp = jnp.where(valid, jnp.where(forced, jnp.inf, imp), -jnp.inf)
    rank = jnp.zeros((nsel, tq), jnp.int32)
    for i in range(nsel):
        row = imp[i:i + 1, :]
        beats = (row > imp) | ((row == imp) & (jj > i))
        rank = rank + jnp.where(beats, 1, 0)
    qbias_ref[0] = jnp.where((rank < topn) & valid, 0.0, MASKED).astype(BF16)


def _nsa_cmp(xr, qbt, pe_lo, pe_hi, w_lo, w_hi, w2c, ovt):
    b, _, _, s = qbt.shape
    nr = s // B_CMP_STRIDE
    nsel = s // B_SLC_LEN
    assert nsel <= LANES - HEAD_DIM, "the selection mask lives in the spare contraction rows of q"
    tq = TOK_TILE
    kern = functools.partial(_cmp_kernel, nsel=nsel, topn=min(B_SLC_TOPN, nsel))
    full = lambda a: pl.BlockSpec(a.shape, lambda i, j: (0,) * a.ndim)
    return pl.pallas_call(
        kern,
        grid=(b, s // tq),
        in_specs=[pl.BlockSpec((1, nr, B_CMP_STRIDE * 128), lambda i, j: (i, 0, 0)),
                  full(pe_lo), full(pe_hi), full(w_lo), full(w_hi), full(w2c),
                  pl.BlockSpec((1, N_HEADS, LANES, tq), lambda i, j: (i, 0, 0, j)), full(ovt)],
        out_specs=[pl.BlockSpec((1, tq, 256), lambda i, j: (i, j, 0)),
                   pl.BlockSpec((1, nsel, tq), lambda i, j: (i, 0, j))],
        out_shape=[jax.ShapeDtypeStruct((b, s, 256), F32),
                   jax.ShapeDtypeStruct((b, nsel, s), BF16)],
        scratch_shapes=[pltpu.VMEM((nr, LANES), BF16), pltpu.VMEM((HEAD_DIM, nr), BF16)],
        compiler_params=_params("parallel", "arbitrary"),
        name="nsa_cmp",
    )(xr, pe_lo, pe_hi, w_lo, w_hi, w2c, qbt, ovt)


def _nsa_kernel(qt_ref, qbias_ref, ksa_ref, kwp_ref, vst_ref, vwt_ref, oslc_ref, owin_ref,
                qc_ref, *two_pass_refs, ntile):
    tq = qt_ref.shape[3]
    tk = tq
    qi = pl.program_id(1)
    q_t = [qt_ref[0, h] for h in range(N_HEADS)]
    key = lax.broadcasted_iota(jnp.int32, (tk, tq), 0)
    qry = lax.broadcasted_iota(jnp.int32, (tk, tq), 1)

    bias_t = qbias_ref[0]
    for h in range(N_HEADS):
        qc_ref[h] = _with_mask_rows(q_t[h], bias_t)

    def scores_of(j):
        kt = ksa_ref[0, pl.ds(pl.multiple_of(j * tk, tk), tk), :]
        return [_dot(kt, qc_ref[h]) for h in range(N_HEADS)]

    def values_of(j):
        return [vst_ref[0, j]] * N_HEADS

    oslc_ref[0] = _heads_to_tokens(
        _attend_two_pass(qi, scores_of, values_of, key <= qry, *two_pass_refs))

    t0 = jnp.maximum(qi + 1 - ntile, 0)

    def window_scores(i):
        kt = kwp_ref[0, pl.ds(pl.multiple_of((t0 + i) * tk, tk), tk), :]
        dist = (qi * tq + qry) - ((t0 + i) * tk + key)
        keep = (dist >= 0) & (dist < B_WINDOW)
        return [jnp.where(keep, _dot(kt, q_t[h]), MASKED) for h in range(N_HEADS)]

    def window_values(i):
        return [vwt_ref[0, t0 + i]] * N_HEADS

    owin_ref[0] = _heads_to_tokens(
        _attend_two_pass(None, window_scores, window_values, None, *two_pass_refs, static_tiles=ntile))


def _nsa_slc_win(qnt, qbias, ksa, kwp, vst, vwt):
    b, s, _ = ksa.shape
    tq = TOK_TILE
    nt = s // tq
    nsel = s // B_SLC_LEN
    ntile = min(B_WINDOW // tq + 1, nt)
    seq = pl.BlockSpec((1, s, 128), lambda i, j: (i, 0, 0))
    v_t = pl.BlockSpec((1, nt, HEAD_DIM, tq), lambda i, j: (i, 0, 0, 0))
    ospec = pl.BlockSpec((1, tq, 256), lambda i, j: (i, j, 0))
    return pl.pallas_call(
        functools.partial(_nsa_kernel, ntile=ntile),
        grid=(b, nt),
        in_specs=[pl.BlockSpec((1, N_HEADS, LANES, tq), lambda i, j: (i, 0, 0, j)),
                  pl.BlockSpec((1, nsel, tq), lambda i, j: (i, 0, j)), seq, seq, v_t, v_t],
        out_specs=[ospec, ospec],
        out_shape=[jax.ShapeDtypeStruct((b, s, 256), F32)] * 2,
        scratch_shapes=[pltpu.VMEM((N_HEADS, LANES, tq), BF16)] + _two_pass_scratch(nt, tq),
        compiler_params=_params("parallel", "arbitrary"),
        name="nsa_slc_win",
    )(qnt, qbias, ksa, kwp, vst, vwt)


def _sb_kernel(q_ref, k_ref, v_ref, o_ref, c_ref, acc_ref):
    tq = q_ref.shape[1]
    tk = tq
    qi = pl.program_id(1)
    q = q_ref[0]
    head = _head_mask((tq, 256))
    qh = [_keep_head(q, head, h) for h in range(N_HEADS)]
    rows = lax.broadcasted_iota(jnp.int32, (tq, tk), 0)
    cols = lax.broadcasted_iota(jnp.int32, (tq, tk), 1)
    after = jnp.where(rows > cols, 1.0, 0.0).astype(BF16)
    after2 = jnp.concatenate([after, after], axis=0)
    c_ref[...] = jnp.zeros(c_ref.shape, F32)
    acc_ref[...] = jnp.zeros(acc_ref.shape, F32)

    def tile(j, strict_causal):
        off = pl.multiple_of(j * tk, tk)
        kt = k_ref[0, pl.ds(off, tk), :]
        vt = v_ref[0, pl.ds(off, tk), :]
        zs = [_dot_nt(qh[h], kt) for h in range(N_HEADS)]
        log_betas, log_1ms, tails = [], [], []
        for z in zs:
            log_beta = jnp.minimum(z, 0.0) - jnp.log(1.0 + jnp.exp(-jnp.abs(z)))
            log_1m = log_beta - z
            if strict_causal is not None:
                log_1m = jnp.where(strict_causal, log_1m, 0.0)
            hi = log_1m.astype(BF16)
            lo = (log_1m - hi.astype(F32)).astype(BF16)
            tails.append(_dot(jnp.concatenate([hi, lo], axis=1), after2))
            log_betas.append(log_beta)
            log_1ms.append(log_1m)
        alive = None
        for h in range(N_HEADS):
            c = c_ref[h]
            a = jnp.exp(log_betas[h] + (tails[h] + c))
            if strict_causal is not None:
                a = jnp.where(strict_causal, a, 0.0)
            acc_ref[h] += _dot(a.astype(BF16), vt)
            c_new = c + jnp.sum(log_1ms[h], axis=1, keepdims=True)
            c_ref[h] = c_new
            alive = c_new if alive is None else jnp.maximum(alive, c_new)
        return jnp.max(alive)

    def cond(carry):
        j, cmax = carry
        return (j >= 1) & (cmax > SB_DEAD)

    def body(carry):
        j, _ = carry
        tile(j, None)
        return j - 2, tile(j - 1, None)

    j, cmax = lax.while_loop(cond, body, (qi - 1, tile(qi, cols < rows)))

    @pl.when((j == 0) & (cmax > SB_DEAD))
    def _():
        tile(0, None)
    out = acc_ref[0]
    for h in range(1, N_HEADS):
        out = jnp.where(head == h, acc_ref[h], out)
    o_ref[0] = out


def _stickbreak(qc, kc, vc):
    b, s, _ = qc.shape
    tq = DENSE_TQ
    seq = pl.BlockSpec((1, s, 256), lambda i, j: (i, 0, 0))
    tile = pl.BlockSpec((1, tq, 256), lambda i, j: (i, j, 0))
    return pl.pallas_call(
        _sb_kernel,
        grid=(b, s // tq),
        in_specs=[tile, seq, seq],
        out_specs=tile,
        out_shape=jax.ShapeDtypeStruct((b, s, 256), F32),
        scratch_shapes=[pltpu.VMEM((N_HEADS, tq, LANES), F32), pltpu.VMEM((N_HEADS, tq, 256), F32)],
        compiler_params=_params("parallel", "arbitrary"),
        name="stickbreak",
    )(qc, kc, vc)


def _dil_kernel(q_ref, k_ref, v_ref, o_ref, lse_ref, *, window, span, tq):
    nq = q_ref.shape[1] // tq
    head = _head_mask((tq, 256))
    rows = lax.broadcasted_iota(jnp.int32, (tq, span), 0)
    cols = lax.broadcasted_iota(jnp.int32, (tq, span), 1)
    qs, ks, vs, keeps = [], [], [], []
    for t in range(nq):
        q0 = (pl.program_id(1) * nq + t) * tq
        k0 = pl.multiple_of(jnp.maximum(q0 + tq - span, 0), tq)
        q = q_ref[0, t * tq:(t + 1) * tq, :]
        dist = (q0 + rows) - (k0 + cols)
        qs += [_keep_head(q, head, h) for h in range(N_HEADS)]
        ks += [k_ref[0, pl.ds(k0, span), :]] * N_HEADS
        vs += [v_ref[0, pl.ds(k0, span), :]] * N_HEADS
        keeps += [(dist >= 0) & (dist <= window)] * N_HEADS
    o, m, l = _softmax_once(qs, ks, vs, keeps)
    for t in range(nq):
        out = o[t * N_HEADS]
        lse = jnp.broadcast_to(m[t * N_HEADS] + jnp.log(l[t * N_HEADS]), (tq, 256))
        for h in range(1, N_HEADS):
            i = t * N_HEADS + h
            out = jnp.where(head == h, o[i], out)
            lse = jnp.where(head == h, m[i] + jnp.log(l[i]), lse)
        o_ref[0, t * tq:(t + 1) * tq, :] = out
        lse_ref[0, t * tq:(t + 1) * tq, :] = lse


def _dilated_group(q, k, v, window):
    n, L, _ = q.shape
    tq = DENSE_TQ
    assert window == tq
    nq = next(n for n in (4, 2, 1) if L % (n * tq) == 0)
    kern = functools.partial(_dil_kernel, window=window, span=min(2 * tq, L), tq=tq)
    seq = pl.BlockSpec((1, L, 256), lambda i, j: (i, 0, 0))
    tile = pl.BlockSpec((1, nq * tq, 256), lambda i, j: (i, j, 0))
    return pl.pallas_call(
        kern,
        grid=(n, L // (nq * tq)),
        in_specs=[tile, seq, seq],
        out_specs=[tile, tile],
        out_shape=[jax.ShapeDtypeStruct((n, L, 256), F32)] * 2,
        compiler_params=_params("parallel", "arbitrary"),
        name="dilated",
    )(q, k, v)


_C_GATE, _C_MERGE, _C_NSAG = 0, 1024, 5120
_W2_COLS = 5248


def _out_kernel(x_ref, nw_ref, w2_ref, oa_ref, ocmp_ref, oslc_ref, owin_ref, oc_ref,
                od0_ref, od1_ref, od2_ref, ls0_ref, ls1_ref, ls2_ref,
                wup_ref, wout_ref, fnw_ref, spread_ref, y_ref, acc_ref, sub_ref, *, final):
    ts = x_ref.shape[1]
    x = x_ref[0]
    h = _rmsnorm(x, nw_ref[...]).astype(BF16)

    def proj(c0, w):
        return _dot(h, w2_ref[:, c0:c0 + w])

    def token_order(ref, dil, slot):
        if dil == 1:
            return ref[0, 0]
        n = ts // dil
        for c in range(dil):
            for half in range(2):
                sub_ref[slot, half, pl.ds(c, n, stride=dil), :] = ref[0, c, :, half * 128:(half + 1) * 128]
        return jnp.concatenate([sub_ref[slot, 0], sub_ref[slot, 1]], axis=1)

    gate = jax.nn.sigmoid(proj(_C_NSAG, 128))
    hi = gate.astype(BF16)
    rest = gate - hi.astype(F32)
    mid = rest.astype(BF16)
    lo = (rest - mid.astype(F32)).astype(BF16)
    spread = spread_ref[...]
    ng = _dot(hi, spread) + _dot(mid, spread) + _dot(lo, spread)
    o_b = ng[:, 0:256] * ocmp_ref[0] + ng[:, 256:512] * oslc_ref[0] + ng[:, 512:768] * owin_ref[0]

    dils = [dil for _, dil in D_PATTERNS]
    od = [token_order(r, d, i) for i, (r, d) in enumerate(zip((od0_ref, od1_ref, od2_ref), dils))]
    ls = [token_order(r, d, 3 + i) for i, (r, d) in enumerate(zip((ls0_ref, ls1_ref, ls2_ref), dils))]
    mx = jnp.maximum(jnp.maximum(ls[0], ls[1]), ls[2])
    e = [jnp.exp(l - mx) for l in ls]
    den = e[0] + e[1] + e[2]
    o_d = (e[0] / den) * od[0] + (e[1] / den) * od[1] + (e[2] / den) * od[2]

    branches = (oa_ref[0], o_b, oc_ref[0], o_d)
    acc_ref[...] = jnp.zeros(acc_ref.shape, F32)
    for i in range(N_BRANCH):
        g = proj(_C_GATE + 256 * i, 256)
        wide = (branches[i] * (g * jax.nn.sigmoid(g))).astype(BF16)
        for half in range(2):
            c0 = 512 * half
            u = _dot(wide, wup_ref[i, :, c0:c0 + 512])
            mg = jax.nn.sigmoid(proj(_C_MERGE + 1024 * i + c0, 512))
            acc_ref[:, c0:c0 + 512] += mg * u
    out = x + _dot(acc_ref[...].astype(BF16), wout_ref[...])
    if final:
        out = _rmsnorm(out, fnw_ref[...])
    y_ref[0] = out


def _merge_out(x, norm_w, w2, token_outs, sub_outs, w_up, w_out, final_norm_w, final):
    b, s, d = x.shape
    ts = TOK_TILE
    tile256 = pl.BlockSpec((1, ts, 256), lambda i, j: (i, j, 0))
    sub_specs = [pl.BlockSpec((1, dil, ts // dil, 256), lambda i, j: (i, 0, j, 0)) for _, dil in D_PATTERNS]
    const = lambda a: pl.BlockSpec(a.shape, lambda i, j: (0,) * a.ndim)
    kern = functools.partial(_out_kernel, final=final)
    gate_id = np.arange(3 * BRANCH_W) // HEAD_DIM
    spread = jnp.asarray(np.arange(128)[:, None] == gate_id[None, :], dtype=BF16)
    return pl.pallas_call(
        kern,
        grid=(b, s // ts),
        in_specs=[pl.BlockSpec((1, ts, d), lambda i, j: (i, j, 0)), const(norm_w), const(w2)]
        + [tile256] * 5 + sub_specs * 2
        + [const(w_up), const(w_out), const(final_norm_w), const(spread)],
        out_specs=pl.BlockSpec((1, ts, d), lambda i, j: (i, j, 0)),
        out_shape=jax.ShapeDtypeStruct((b, s, d), F32),
        scratch_shapes=[pltpu.VMEM((ts, d), F32), pltpu.VMEM((6, 2, ts, 128), F32)],
        compiler_params=_params("parallel", "arbitrary"),
        name="merge_out",
    )(x, norm_w, w2, *token_outs, *sub_outs, w_up, w_out, final_norm_w, spread)


def _rope_tables(s):
    half = HEAD_DIM // 2
    inv = ROPE_THETA ** (-jnp.arange(half, dtype=F32) / half)
    ang = jnp.arange(s).astype(F32)[:, None] * inv[None, :]
    cos, sin = jnp.cos(ang), jnp.sin(ang)
    cos_t = jnp.tile(jnp.concatenate([cos, cos], axis=-1), (1, N_HEADS))
    sin_t = jnp.tile(jnp.concatenate([-sin, sin], axis=-1), (1, N_HEADS))
    return cos_t, sin_t


def _layer_weights(w_in, cmp_pos, cmp_w1, cmp_w2):
    o = np.concatenate([[0], np.cumsum(IN_SIZES)])
    qa, ka, va, ga, qb, kvb, gb, nsag, qkvc, gc, qkvd, gd, merge = [int(v) for v in o[:-1]]
    cols = lambda a, w: w_in[:, a:a + w]
    hd = HEAD_DIM
    w1 = jnp.concatenate([
        cols(qa, 256), cols(ka, 256), cols(va, 256), cols(qb, 256),
        cols(kvb + 2 * hd, hd), cols(kvb + 4 * hd, hd),
        cols(kvb, 2 * hd),
        cols(kvb + 3 * hd, hd), cols(kvb + 5 * hd, hd),
        cols(qkvc, 768), cols(qkvd, 2304)], axis=1)
    gate_cols = jnp.concatenate([cols(ga, 256), cols(gb, 256), cols(gc, 256), cols(gd, 256)], axis=1)
    nsag_cols = jnp.pad(cols(nsag, 3 * N_HEADS), ((0, 0), (0, 128 - 3 * N_HEADS)))
    w2 = jnp.concatenate([gate_cols, cols(merge, 4096), nsag_cols], axis=1)

    half = B_CMP_STRIDE
    pe = jnp.concatenate([cmp_pos[0], cmp_pos[1]], axis=-1)
    pe_lo = pe[:half].reshape(1, half * 128)
    pe_hi = pe[half:].reshape(1, half * 128)
    w1k = cmp_w1[0].reshape(B_CMP_LEN, hd, B_CMP_HIDDEN)
    w1v = cmp_w1[1].reshape(B_CMP_LEN, hd, B_CMP_HIDDEN)
    zw = jnp.zeros_like(w1k)
    w1_big = jnp.concatenate([jnp.concatenate([w1k, zw], axis=2),
                              jnp.concatenate([zw, w1v], axis=2)], axis=1)
    w_lo = w1_big[:half].reshape(half * 128, 2 * B_CMP_HIDDEN).astype(BF16)
    w_hi = w1_big[half:].reshape(half * 128, 2 * B_CMP_HIDDEN).astype(BF16)

    w2c = jnp.zeros((2 * B_CMP_HIDDEN, 2 * LANES), F32)
    w2c = w2c.at[:B_CMP_HIDDEN, 0:hd].set(cmp_w2[0])
    w2c = w2c.at[B_CMP_HIDDEN:, LANES:LANES + hd].set(cmp_w2[1])

    return w1, w2, pe_lo, pe_hi, w_lo, w_hi, w2c.astype(BF16)


def _overlap_t(s):
    nr = s // B_CMP_STRIDE
    nc = nr - 1
    nsel = s // B_SLC_LEN
    starts = np.arange(nr) * B_CMP_STRIDE
    j = np.arange(nsel)
    ov = ((starts[None, :] < (j[:, None] + 1) * B_SLC_LEN)
          & (starts[None, :] + B_CMP_LEN > j[:, None] * B_SLC_LEN)
          & (np.arange(nr)[None, :] < nc))
    return jnp.asarray(ov.astype(np.float32)).astype(BF16)


def _layer(x, norm_w, w_in, cmp_pos, cmp_w1, cmp_w2, w_up, w_out, final_norm_w, final, tables, ovt):
    b, s, _ = x.shape
    w1, w2, pe_lo, pe_hi, w_lo, w_hi, w2c = _layer_weights(w_in, cmp_pos, cmp_w1, cmp_w2)
    nw = norm_w.reshape(1, -1)
    (qat, ka, vat, kmean, qbt, qnt, ksa, kwp, vst, vwt, kcvc, qc, kc, vc, *qkvd) = _inproj(x, nw, w1, *tables)

    o_a = _moba(qat, ka, vat, kmean.transpose(0, 2, 1, 3))
    o_cmp, qbias = _nsa_cmp(kcvc, qbt, pe_lo, pe_hi, w_lo, w_hi, w2c, ovt)
    o_slc, o_win = _nsa_slc_win(qnt, qbias, ksa, kwp, vst, vwt)
    o_c = _stickbreak(qc, kc, vc)

    od, ls = [], []
    for g, (window, dil) in enumerate(D_PATTERNS):
        q, k, v = (a.reshape(b * dil, s // dil, 256) for a in qkvd[3 * g:3 * g + 3])
        o, l = _dilated_group(q, k, v, window // dil)
        od.append(o.reshape(b, dil, s // dil, 256))
        ls.append(l.reshape(b, dil, s // dil, 256))

    return _merge_out(x, nw, w2, (o_a, o_cmp, o_slc, o_win, o_c), (*od, *ls),
                      w_up.astype(BF16), w_out.astype(BF16), final_norm_w.reshape(1, -1), final)


def kernel(x, norm_w, w_in, nsa_cmp_pos, nsa_cmp_w1, nsa_cmp_w2, w_up, w_out, final_norm_w):
    depth = norm_w.shape[0]
    s = x.shape[1]
    tables = _rope_tables(s)
    ovt = _overlap_t(s)
    w_in = w_in.astype(BF16)
    for layer in range(depth):
        x = _layer(x, norm_w[layer], w_in[layer], nsa_cmp_pos[layer], nsa_cmp_w1[layer],
                   nsa_cmp_w2[layer], w_up[layer], w_out[layer], final_norm_w,
                   layer == depth - 1, tables, ovt)
    return x
```

```python
import functools

import numpy as np
import jax
import jax.numpy as jnp
from jax import lax
from jax.experimental import pallas as pl
from jax.experimental.pallas import tpu as pltpu

F32 = jnp.float32
BF16 = jnp.bfloat16

HEAD_DIM = 64
N_HEADS = 4
BRANCH_W = N_HEADS * HEAD_DIM
ROPE_THETA = 10000.0
NORM_EPS = 1e-6
QK_SCALE = HEAD_DIM ** -0.5

A_BLOCK = 256
A_SHIFT = 8
A_TOPK = 3
B_CMP_LEN = 32
B_CMP_STRIDE = 16
B_CMP_HIDDEN = 256
B_SLC_LEN = 64
B_SLC_SHIFT = 6
B_SLC_TOPN = 16
B_WINDOW = 512
D_PATTERNS = ((128, 1), (512, 4), (2048, 16))
N_BRANCH = 4

IN_SIZES = (
    BRANCH_W, BRANCH_W, BRANCH_W, BRANCH_W,
    BRANCH_W, 6 * HEAD_DIM, BRANCH_W, 3 * N_HEADS,
    3 * BRANCH_W, BRANCH_W,
    3 * len(D_PATTERNS) * N_HEADS * HEAD_DIM, BRANCH_W,
    N_BRANCH * 1024,
)

MASKED = -2e30
LOG2_E = 1.4426950408889634
SB_DEAD = -120.0
VMEM_LIMIT = 56 * 1024 * 1024
LANES = 128

TOK_TILE = 256
DENSE_TQ = 128


def _dot(a, b):
    return jnp.dot(a, b, preferred_element_type=F32)


def _dot_nt(a, b):
    return lax.dot_general(a, b, (((1,), (1,)), ((), ())), preferred_element_type=F32)


def _params(*sem):
    return pltpu.CompilerParams(dimension_semantics=sem, vmem_limit_bytes=VMEM_LIMIT)


def _rmsnorm(x, w):
    y = x * lax.rsqrt(jnp.mean(x * x, axis=-1, keepdims=True) + NORM_EPS)
    return y * w


def _head_mask(shape):
    return lax.broadcasted_iota(jnp.int32, shape, 1) >> 6


def _keep_head(q, head, h):
    return jnp.where(head == h, q.astype(F32), 0.0).astype(BF16)


def _with_mask_rows(q_t, bias_t):
    n, tq = bias_t.shape
    parts = [q_t[:HEAD_DIM], bias_t.astype(BF16)]
    if HEAD_DIM + n < LANES:
        parts.append(jnp.zeros((LANES - HEAD_DIM - n, tq), BF16))
    return jnp.concatenate(parts, axis=0)


def _heads_to_tokens(outs_t):
    return jnp.transpose(jnp.concatenate(outs_t, axis=0))


def _two_pass_scratch(n_tiles, tq):
    return [pltpu.VMEM((N_HEADS, n_tiles, tq, tq), F32), pltpu.VMEM((N_HEADS, HEAD_DIM, tq), F32),
            pltpu.VMEM((N_HEADS, 8, tq), F32), pltpu.VMEM((N_HEADS, 8, tq), F32)]


def _attend_two_pass(n_full, scores_of, values_of, keep_last, s_ref, acc_ref, peak_ref, total_ref,
                     static_tiles=0):
    heads, _, tk, tq = s_ref.shape

    def fold(x, op):
        return op(x.reshape(tk // 8, 8, tq), axis=0)

    def for_tiles(count, fn):
        def quad(i, carry):
            for t in range(4):
                fn(4 * i + t)
            return carry

        lax.fori_loop(0, count // 4, quad, 0)
        done = (count // 4) * 4

        @pl.when((count & 2) != 0)
        def _():
            fn(done)
            fn(done + 1)

        @pl.when((count & 1) != 0)
        def _():
            fn(count - 1)

    def park(j, scores):
        for h, s in enumerate(scores):
            s = s * LOG2_E
            s_ref[h, j] = s
            peak_ref[h] = jnp.maximum(peak_ref[h], fold(s, jnp.max))

    peak_ref[...] = jnp.full(peak_ref.shape, MASKED, F32)
    if static_tiles:
        for j in range(static_tiles):
            park(j, scores_of(j))
    else:
        for_tiles(n_full, lambda j: park(j, scores_of(j)))
        park(n_full, [jnp.where(keep_last, s, MASKED) for s in scores_of(n_full)])
    top = [jnp.max(peak_ref[h], axis=0, keepdims=True) for h in range(heads)]

    acc_ref[...] = jnp.zeros(acc_ref.shape, F32)
    total_ref[...] = jnp.zeros(total_ref.shape, F32)

    def weigh(j):
        for h, v_t in enumerate(values_of(j)):
            p = jnp.exp2(s_ref[h, j] - top[h])
            total_ref[h] += fold(p, jnp.sum)
            acc_ref[h] += _dot(v_t, p.astype(BF16))

    if static_tiles:
        for j in range(static_tiles):
            weigh(j)
    else:
        for_tiles(n_full + 1, weigh)
    return [acc_ref[h] / jnp.sum(total_ref[h], axis=0, keepdims=True) for h in range(heads)]


def _softmax_once(qs, ks, vs, keeps):
    scores = [jnp.where(keep, _dot_nt(q, k), MASKED) for q, k, keep in zip(qs, ks, keeps)]
    outs, maxes, sums = [], [], []
    for s, v in zip(scores, vs):
        m = jnp.max(s, axis=1, keepdims=True)
        p = jnp.exp(s - m)
        l = jnp.sum(p, axis=1, keepdims=True)
        outs.append(_dot(p.astype(BF16), v) / l)
        maxes.append(m)
        sums.append(l)
    return outs, maxes, sums


_C_QA, _C_KA, _C_VA, _C_QB = 0, 256, 512, 768
_C_KSKW, _C_KCVC, _C_VSVW = 1024, 1152, 1280
_C_QC, _C_KC, _C_VC = 1408, 1664, 1920
_C_QD, _C_KD, _C_VD = 2176, 2944, 3712
_W1_COLS = 4480


def _inproj_kernel(x_ref, nw_ref, w_ref, cos_ref, sin_ref,
                   qat_ref, ka_ref, vat_ref, kmean_ref,
                   qbt_ref, qnt_ref, ksa_ref, kwp_ref, vst_ref, vwt_ref, kcvc_ref,
                   qc_ref, kc_ref, vc_ref,
                   qd0_ref, kd0_ref, vd0_ref, qd1_ref, kd1_ref, vd1_ref, qd2_ref, kd2_ref, vd2_ref,
                   sub_ref):
    ts = x_ref.shape[1]
    pos0 = pl.program_id(1) * ts
    h = _rmsnorm(x_ref[0], nw_ref[...]).astype(BF16)
    cos = cos_ref[...]
    sin = sin_ref[...]
    lane = lax.broadcasted_iota(jnp.int32, (ts, 256), 1)
    first = (lane & 63) < 32
    lane128 = lax.broadcasted_iota(jnp.int32, (ts, 128), 1)
    pos128 = pos0 + lax.broadcasted_iota(jnp.int32, (ts, 128), 0)
    first128 = (lane128 & 63) < 32
    lo128 = lane128 < 64

    def proj(c0, w=256):
        return _dot(h, w_ref[:, c0:c0 + w])

    def rope(v):
        partner = jnp.where(first, pltpu.roll(v, 224, 1), pltpu.roll(v, 32, 1))
        return v * cos + partner * sin

    def split_heads(v):
        out = []
        for pair in range(2):
            p = v[:, pair * 128:(pair + 1) * 128]
            out.append(jnp.where(lo128, p, 0.0))
            out.append(jnp.where(lo128, pltpu.roll(p, 64, 1), 0.0))
        return out

    def store_q_t(ref, v):
        v_t = jnp.transpose(v)
        for hd in range(N_HEADS):
            ref[0, hd, 0:HEAD_DIM, :] = v_t[hd * HEAD_DIM:(hd + 1) * HEAD_DIM].astype(BF16)
            ref[0, hd, HEAD_DIM:LANES, :] = jnp.zeros((LANES - HEAD_DIM, ts), BF16)

    store_q_t(qat_ref, rope(proj(_C_QA)) * QK_SCALE)
    ka = split_heads(rope(proj(_C_KA)))
    va_t = jnp.transpose(proj(_C_VA))
    block_hot = jnp.where(lane128 - 64 == (pos128 >> A_SHIFT), 1.0, 0.0)
    for hd in range(N_HEADS):
        ka_ref[0, hd] = (ka[hd] + block_hot).astype(BF16)
        vat_ref[0, hd, 0] = va_t[hd * HEAD_DIM:(hd + 1) * HEAD_DIM].astype(BF16)
        kmean_ref[0, 0, pl.ds(hd, 1), :] = jnp.sum(ka[hd], axis=0, keepdims=True) * (1.0 / ts)

    qb = proj(_C_QB)
    store_q_t(qbt_ref, qb * QK_SCALE)
    store_q_t(qnt_ref, rope(qb) * QK_SCALE)
    kskw = proj(_C_KSKW, 128)
    partner = jnp.where(first128, pltpu.roll(kskw, 96, 1), pltpu.roll(kskw, 32, 1))
    kskw = kskw * cos[:, :128] + partner * sin[:, :128]
    slc_hot = jnp.where(lane128 - 64 == (pos128 >> B_SLC_SHIFT), 1.0, 0.0)
    ksa_ref[0] = jnp.where(lo128, kskw, slc_hot).astype(BF16)
    kwp_ref[0] = jnp.where(lo128, pltpu.roll(kskw, 64, 1), 0.0).astype(BF16)
    vsvw_t = jnp.transpose(proj(_C_VSVW, 128))
    vst_ref[0, 0] = vsvw_t[0:HEAD_DIM].astype(BF16)
    vwt_ref[0, 0] = vsvw_t[HEAD_DIM:2 * HEAD_DIM].astype(BF16)
    sub_ref[0] = proj(_C_KCVC, 128)
    for t in range(B_CMP_STRIDE):
        kcvc_ref[0, :, t * 128:(t + 1) * 128] = sub_ref[0, pl.ds(t, ts // B_CMP_STRIDE, stride=B_CMP_STRIDE), :]

    qc_ref[0] = (proj(_C_QC) * QK_SCALE).astype(BF16)
    kc_ref[0] = proj(_C_KC).astype(BF16)
    vc_ref[0] = proj(_C_VC).astype(BF16)

    def store_sub(ref, v, dil):
        if dil == 1:
            ref[0, 0] = v.astype(BF16)
            return
        n = ts // dil
        for half in range(2):
            sub_ref[half] = v[:, half * 128:(half + 1) * 128]
        for c in range(dil):
            for half in range(2):
                ref[0, c, :, half * 128:(half + 1) * 128] = (
                    sub_ref[half, pl.ds(c, n, stride=dil), :].astype(BF16))

    groups = ((qd0_ref, kd0_ref, vd0_ref), (qd1_ref, kd1_ref, vd1_ref), (qd2_ref, kd2_ref, vd2_ref))
    for g, (q_ref, k_ref, v_ref) in enumerate(groups):
        dil = D_PATTERNS[g][1]
        store_sub(q_ref, rope(proj(_C_QD + 256 * g)) * QK_SCALE, dil)
        store_sub(k_ref, rope(proj(_C_KD + 256 * g)), dil)
        store_sub(v_ref, proj(_C_VD + 256 * g), dil)


def _inproj(x, norm_w, w1, cos_t, sin_t):
    b, s, d = x.shape
    ts = TOK_TILE
    nt = s // ts
    tok = lambda w, dt: jax.ShapeDtypeStruct((b, s, w), dt)
    tok_spec = lambda w: pl.BlockSpec((1, ts, w), lambda i, j: (i, j, 0))
    head = jax.ShapeDtypeStruct((b, N_HEADS, s, 128), BF16)
    head_spec = pl.BlockSpec((1, N_HEADS, ts, 128), lambda i, j: (i, 0, j, 0))
    q_t = jax.ShapeDtypeStruct((b, N_HEADS, LANES, s), BF16)
    q_t_spec = pl.BlockSpec((1, N_HEADS, LANES, ts), lambda i, j: (i, 0, 0, j))
    v_t = jax.ShapeDtypeStruct((b, nt, HEAD_DIM, ts), BF16)
    v_t_spec = pl.BlockSpec((1, 1, HEAD_DIM, ts), lambda i, j: (i, j, 0, 0))
    nrow = ts // B_CMP_STRIDE
    out_shape = [q_t, head, jax.ShapeDtypeStruct((b, N_HEADS, nt, HEAD_DIM, ts), BF16),
                 jax.ShapeDtypeStruct((b, nt, N_HEADS, 128), F32),
                 q_t, q_t, tok(128, BF16), tok(128, BF16), v_t, v_t,
                 jax.ShapeDtypeStruct((b, s // B_CMP_STRIDE, B_CMP_STRIDE * 128), F32),
                 tok(256, BF16), tok(256, BF16), tok(256, BF16)]
    out_specs = [q_t_spec, head_spec,
                 pl.BlockSpec((1, N_HEADS, 1, HEAD_DIM, ts), lambda i, j: (i, 0, j, 0, 0)),
                 pl.BlockSpec((1, 1, N_HEADS, 128), lambda i, j: (i, j, 0, 0)),
                 q_t_spec, q_t_spec, tok_spec(128), tok_spec(128), v_t_spec, v_t_spec,
                 pl.BlockSpec((1, nrow, B_CMP_STRIDE * 128), lambda i, j: (i, j, 0)),
                 tok_spec(256), tok_spec(256), tok_spec(256)]
    for _, dil in D_PATTERNS:
        out_shape += [jax.ShapeDtypeStruct((b, dil, s // dil, 256), BF16)] * 3
        out_specs += [pl.BlockSpec((1, dil, ts // dil, 256), lambda i, j: (i, 0, j, 0))] * 3
    return pl.pallas_call(
        _inproj_kernel,
        grid=(b, nt),
        in_specs=[
            pl.BlockSpec((1, ts, d), lambda i, j: (i, j, 0)),
            pl.BlockSpec((1, d), lambda i, j: (0, 0)),
            pl.BlockSpec((d, _W1_COLS), lambda i, j: (0, 0)),
            pl.BlockSpec((ts, 256), lambda i, j: (j, 0)),
            pl.BlockSpec((ts, 256), lambda i, j: (j, 0)),
        ],
        out_specs=out_specs,
        out_shape=out_shape,
        scratch_shapes=[pltpu.VMEM((2, ts, 128), F32)],
        compiler_params=_params("parallel", "arbitrary"),
        name="inproj",
    )(x, norm_w, w1, cos_t, sin_t)


def _moba_kernel(qt_ref, k_ref, vt_ref, km_ref, o_ref, qc_ref, *two_pass_refs, nb, topk):
    tq = qt_ref.shape[3]
    tk = tq
    qb = pl.program_id(1)
    blk_t = lax.broadcasted_iota(jnp.int32, (nb, tq), 0)
    past_t = blk_t < qb
    key = lax.broadcasted_iota(jnp.int32, (tk, tq), 0)
    qry = lax.broadcasted_iota(jnp.int32, (tk, tq), 1)
    causal = key <= qry

    for h in range(N_HEADS):
        q_t = qt_ref[0, h]
        gate = jnp.where(past_t, _dot(km_ref[0, h].astype(BF16), q_t), -jnp.inf)
        rank = jnp.zeros((nb, tq), jnp.int32)
        for i in range(nb):
            row = gate[i:i + 1, :]
            beats = (row > gate) | ((row == gate) & (blk_t > i))
            rank = rank + jnp.where(beats, 1, 0)
        keep = ((rank < topk) & past_t) | (blk_t == qb)
        qc_ref[h] = _with_mask_rows(q_t, jnp.where(keep, 0.0, MASKED))

    def scores_of(j):
        off = pl.multiple_of(j * tk, tk)
        return [_dot(k_ref[0, h, pl.ds(off, tk), :], qc_ref[h]) for h in range(N_HEADS)]

    def values_of(j):
        return [vt_ref[0, h, j] for h in range(N_HEADS)]

    o_ref[0] = _heads_to_tokens(_attend_two_pass(qb, scores_of, values_of, causal, *two_pass_refs))


def _moba(qat, ka, vat, kmean):
    b, _, s, _ = ka.shape
    tq = A_BLOCK
    nb = s // A_BLOCK
    kern = functools.partial(_moba_kernel, nb=nb, topk=min(A_TOPK, nb))
    return pl.pallas_call(
        kern,
        grid=(b, nb),
        in_specs=[pl.BlockSpec((1, N_HEADS, LANES, tq), lambda i, j: (i, 0, 0, j)),
                  pl.BlockSpec((1, N_HEADS, s, 128), lambda i, j: (i, 0, 0, 0)),
                  pl.BlockSpec((1, N_HEADS, nb, HEAD_DIM, tq), lambda i, j: (i, 0, 0, 0, 0)),
                  pl.BlockSpec((1, N_HEADS, nb, 128), lambda i, j: (i, 0, 0, 0))],
        out_specs=pl.BlockSpec((1, tq, 256), lambda i, j: (i, j, 0)),
        out_shape=jax.ShapeDtypeStruct((b, s, 256), F32),
        scratch_shapes=[pltpu.VMEM((N_HEADS, LANES, tq), BF16)] + _two_pass_scratch(nb, tq),
        compiler_params=_params("parallel", "arbitrary"),
        name="moba",
    )(qat, ka, vat, kmean)


def _gelu_tanh(x):
    return 0.5 * x * (1.0 + jnp.tanh(np.sqrt(2.0 / np.pi).astype(np.float32) * (x + 0.044715 * (x * x * x))))


def _cmp_kernel(xr_ref, pelo_ref, pehi_ref, wlo_ref, whi_ref, w2_ref, qt_ref, ovt_ref,
                o_ref, qbias_ref, kc_ref, vct_ref, *, nsel, topn):
    tq = qt_ref.shape[3]
    nr = xr_ref.shape[1]
    nc = nr - 1
    qi = pl.program_id(1)

    @pl.when(qi == 0)
    def _():
        xr = xr_ref[0]
        lo = (xr + pelo_ref[...]).astype(BF16)
        hi = (pltpu.roll(xr, nr - 1, 0) + pehi_ref[...]).astype(BF16)
        hid = _dot(lo, wlo_ref[...]) + _dot(hi, whi_ref[...])
        act = _gelu_tanh(hid).astype(BF16)
        kv = _dot(act, w2_ref[...])
        kc_ref[...] = kv[:, 0:LANES].astype(BF16)
        vct_ref[...] = jnp.transpose(kv[:, LANES:2 * LANES])[0:HEAD_DIM].astype(BF16)

    tpos = qi * tq + lax.broadcasted_iota(jnp.int32, (nr, tq), 1)
    n = lax.broadcasted_iota(jnp.int32, (nr, tq), 0)
    vis = (n * B_CMP_STRIDE + (B_CMP_LEN - 1) <= tpos) & (n < nc)
    ovt = ovt_ref[...]
    kc = kc_ref[...]
    vct = vct_ref[...]
    scores = [jnp.where(vis, _dot(kc, qt_ref[0, h]), MASKED) for h in range(N_HEADS)]
    outs = []
    imp = jnp.zeros((nsel, tq), F32)
    for s in scores:
        mx = jnp.max(s, axis=0, keepdims=True)
        e = jnp.where(vis, jnp.exp(s - mx), 0.0)
        den = jnp.maximum(jnp.sum(e, axis=0, keepdims=True), 1e-30)
        p = (e / den).astype(BF16)
        outs.append(_dot(vct, p))
        imp = imp + _dot(ovt, p)
    o_ref[0] = _heads_to_tokens(outs)

    jj = lax.broadcasted_iota(jnp.int32, (nsel, tq), 0)
    cur = (qi * tq + lax.broadcasted_iota(jnp.int32, (nsel, tq), 1)) >> B_SLC_SHIFT
    forced = (jj == 0) | (jj == cur) | (jj == cur - 1)
    valid = jj <= cur
    imp = jnp.where(valid, jnp.where(forced, jnp.inf, imp), -jnp.inf)
    rank = jnp.zeros((nsel, tq), jnp.int32)
    for i in range(nsel):
        row = imp[i:i + 1, :]
        beats = (row > imp) | ((row == imp) & (jj > i))
        rank = rank + jnp.where(beats, 1, 0)
    qbias_ref[0] = jnp.where((rank < topn) & valid, 0.0, MASKED).astype(BF16)


def _nsa_cmp(xr, qbt, pe_lo, pe_hi, w_lo, w_hi, w2c, ovt):
    b, _, _, s = qbt.shape
    nr = s // B_CMP_STRIDE
    nsel = s // B_SLC_LEN
    assert nsel <= LANES - HEAD_DIM, "the selection mask lives in the spare contraction rows of q"
    tq = TOK_TILE
    kern = functools.partial(_cmp_kernel, nsel=nsel, topn=min(B_SLC_TOPN, nsel))
    full = lambda a: pl.BlockSpec(a.shape, lambda i, j: (0,) * a.ndim)
    return pl.pallas_call(
        kern,
        grid=(b, s // tq),
        in_specs=[pl.BlockSpec((1, nr, B_CMP_STRIDE * 128), lambda i, j: (i, 0, 0)),
                  full(pe_lo), full(pe_hi), full(w_lo), full(w_hi), full(w2c),
                  pl.BlockSpec((1, N_HEADS, LANES, tq), lambda i, j: (i, 0, 0, j)), full(ovt)],
        out_specs=[pl.BlockSpec((1, tq, 256), lambda i, j: (i, j, 0)),
                   pl.BlockSpec((1, nsel, tq), lambda i, j: (i, 0, j))],
        out_shape=[jax.ShapeDtypeStruct((b, s, 256), F32),
                   jax.ShapeDtypeStruct((b, nsel, s), BF16)],
        scratch_shapes=[pltpu.VMEM((nr, LANES), BF16), pltpu.VMEM((HEAD_DIM, nr), BF16)],
        compiler_params=_params("parallel", "arbitrary"),
        name="nsa_cmp",
    )(xr, pe_lo, pe_hi, w_lo, w_hi, w2c, qbt, ovt)


def _nsa_kernel(qt_ref, qbias_ref, ksa_ref, kwp_ref, vst_ref, vwt_ref, oslc_ref, owin_ref,
                qc_ref, *two_pass_refs, ntile):
    tq = qt_ref.shape[3]
    tk = tq
    qi = pl.program_id(1)
    q_t = [qt_ref[0, h] for h in range(N_HEADS)]
    key = lax.broadcasted_iota(jnp.int32, (tk, tq), 0)
    qry = lax.broadcasted_iota(jnp.int32, (tk, tq), 1)

    bias_t = qbias_ref[0]
    for h in range(N_HEADS):
        qc_ref[h] = _with_mask_rows(q_t[h], bias_t)

    def scores_of(j):
        kt = ksa_ref[0, pl.ds(pl.multiple_of(j * tk, tk), tk), :]
        return [_dot(kt, qc_ref[h]) for h in range(N_HEADS)]

    def values_of(j):
        return [vst_ref[0, j]] * N_HEADS

    oslc_ref[0] = _heads_to_tokens(
        _attend_two_pass(qi, scores_of, values_of, key <= qry, *two_pass_refs))

    t0 = jnp.maximum(qi + 1 - ntile, 0)

    def window_scores(i):
        kt = kwp_ref[0, pl.ds(pl.multiple_of((t0 + i) * tk, tk), tk), :]
        dist = (qi * tq + qry) - ((t0 + i) * tk + key)
        keep = (dist >= 0) & (dist < B_WINDOW)
        return [jnp.where(keep, _dot(kt, q_t[h]), MASKED) for h in range(N_HEADS)]

    def window_values(i):
        return [vwt_ref[0, t0 + i]] * N_HEADS

    owin_ref[0] = _heads_to_tokens(
        _attend_two_pass(None, window_scores, window_values, None, *two_pass_refs, static_tiles=ntile))


def _nsa_slc_win(qnt, qbias, ksa, kwp, vst, vwt):
    b, s, _ = ksa.shape
    tq = TOK_TILE
    nt = s // tq
    nsel = s // B_SLC_LEN
    ntile = min(B_WINDOW // tq + 1, nt)
    seq = pl.BlockSpec((1, s, 128), lambda i, j: (i, 0, 0))
    v_t = pl.BlockSpec((1, nt, HEAD_DIM, tq), lambda i, j: (i, 0, 0, 0))
    ospec = pl.BlockSpec((1, tq, 256), lambda i, j: (i, j, 0))
    return pl.pallas_call(
        functools.partial(_nsa_kernel, ntile=ntile),
        grid=(b, nt),
        in_specs=[pl.BlockSpec((1, N_HEADS, LANES, tq), lambda i, j: (i, 0, 0, j)),
                  pl.BlockSpec((1, nsel, tq), lambda i, j: (i, 0, j)), seq, seq, v_t, v_t],
        out_specs=[ospec, ospec],
        out_shape=[jax.ShapeDtypeStruct((b, s, 256), F32)] * 2,
        scratch_shapes=[pltpu.VMEM((N_HEADS, LANES, tq), BF16)] + _two_pass_scratch(nt, tq),
        compiler_params=_params("parallel", "arbitrary"),
        name="nsa_slc_win",
    )(qnt, qbias, ksa, kwp, vst, vwt)


def _sb_kernel(q_ref, k_ref, v_ref, o_ref, c_ref, acc_ref):
    tq = q_ref.shape[1]
    tk = tq
    qi = pl.program_id(1)
    q = q_ref[0]
    head = _head_mask((tq, 256))
    qh = [_keep_head(q, head, h) for h in range(N_HEADS)]
    rows = lax.broadcasted_iota(jnp.int32, (tq, tk), 0)
    cols = lax.broadcasted_iota(jnp.int32, (tq, tk), 1)
    after = jnp.where(rows > cols, 1.0, 0.0).astype(BF16)
    after2 = jnp.concatenate([after, after], axis=0)
    c_ref[...] = jnp.zeros(c_ref.shape, F32)
    acc_ref[...] = jnp.zeros(acc_ref.shape, F32)

    def tile(j, strict_causal):
        off = pl.multiple_of(j * tk, tk)
        kt = k_ref[0, pl.ds(off, tk), :]
        vt = v_ref[0, pl.ds(off, tk), :]
        zs = [_dot_nt(qh[h], kt) for h in range(N_HEADS)]
        log_betas, log_1ms, tails = [], [], []
        for z in zs:
            log_beta = jnp.minimum(z, 0.0) - jnp.log(1.0 + jnp.exp(-jnp.abs(z)))
            log_1m = log_beta - z
            if strict_causal is not None:
                log_1m = jnp.where(strict_causal, log_1m, 0.0)
            hi = log_1m.astype(BF16)
            lo = (log_1m - hi.astype(F32)).astype(BF16)
            tails.append(_dot(jnp.concatenate([hi, lo], axis=1), after2))
            log_betas.append(log_beta)
            log_1ms.append(log_1m)
        alive = None
        for h in range(N_HEADS):
            c = c_ref[h]
            a = jnp.exp(log_betas[h] + (tails[h] + c))
            if strict_causal is not None:
                a = jnp.where(strict_causal, a, 0.0)
            acc_ref[h] += _dot(a.astype(BF16), vt)
            c_new = c + jnp.sum(log_1ms[h], axis=1, keepdims=True)
            c_ref[h] = c_new
            alive = c_new if alive is None else jnp.maximum(alive, c_new)
        return jnp.max(alive)

    def cond(carry):
        j, cmax = carry
        return (j >= 1) & (cmax > SB_DEAD)

    def body(carry):
        j, _ = carry
        tile(j, None)
        return j - 2, tile(j - 1, None)

    j, cmax = lax.while_loop(cond, body, (qi - 1, tile(qi, cols < rows)))

    @pl.when((j == 0) & (cmax > SB_DEAD))
    def _():
        tile(0, None)
    out = acc_ref[0]
    for h in range(1, N_HEADS):
        out = jnp.where(head == h, acc_ref[h], out)
    o_ref[0] = out


def _stickbreak(qc, kc, vc):
    b, s, _ = qc.shape
    tq = DENSE_TQ
    seq = pl.BlockSpec((1, s, 256), lambda i, j: (i, 0, 0))
    tile = pl.BlockSpec((1, tq, 256), lambda i, j: (i, j, 0))
    return pl.pallas_call(
        _sb_kernel,
        grid=(b, s // tq),
        in_specs=[tile, seq, seq],
        out_specs=tile,
        out_shape=jax.ShapeDtypeStruct((b, s, 256), F32),
        scratch_shapes=[pltpu.VMEM((N_HEADS, tq, LANES), F32), pltpu.VMEM((N_HEADS, tq, 256), F32)],
        compiler_params=_params("parallel", "arbitrary"),
        name="stickbreak",
    )(qc, kc, vc)


def _dil_kernel(q_ref, k_ref, v_ref, o_ref, lse_ref, *, window, span, tq):
    nq = q_ref.shape[1] // tq
    head = _head_mask((tq, 256))
    rows = lax.broadcasted_iota(jnp.int32, (tq, span), 0)
    cols = lax.broadcasted_iota(jnp.int32, (tq, span), 1)
    qs, ks, vs, keeps = [], [], [], []
    for t in range(nq):
        q0 = (pl.program_id(1) * nq + t) * tq
        k0 = pl.multiple_of(jnp.maximum(q0 + tq - span, 0), tq)
        q = q_ref[0, t * tq:(t + 1) * tq, :]
        dist = (q0 + rows) - (k0 + cols)
        qs += [_keep_head(q, head, h) for h in range(N_HEADS)]
        ks += [k_ref[0, pl.ds(k0, span), :]] * N_HEADS
        vs += [v_ref[0, pl.ds(k0, span), :]] * N_HEADS
        keeps += [(dist >= 0) & (dist <= window)] * N_HEADS
    o, m, l = _softmax_once(qs, ks, vs, keeps)
    for t in range(nq):
        out = o[t * N_HEADS]
        lse = jnp.broadcast_to(m[t * N_HEADS] + jnp.log(l[t * N_HEADS]), (tq, 256))
        for h in range(1, N_HEADS):
            i = t * N_HEADS + h
            out = jnp.where(head == h, o[i], out)
            lse = jnp.where(head == h, m[i] + jnp.log(l[i]), lse)
        o_ref[0, t * tq:(t + 1) * tq, :] = out
        lse_ref[0, t * tq:(t + 1) * tq, :] = lse


def _dilated_group(q, k, v, window):
    n, L, _ = q.shape
    tq = DENSE_TQ
    assert window == tq
    nq = next(n for n in (4, 2, 1) if L % (n * tq) == 0)
    kern = functools.partial(_dil_kernel, window=window, span=min(2 * tq, L), tq=tq)
    seq = pl.BlockSpec((1, L, 256), lambda i, j: (i, 0, 0))
    tile = pl.BlockSpec((1, nq * tq, 256), lambda i, j: (i, j, 0))
    return pl.pallas_call(
        kern,
        grid=(n, L // (nq * tq)),
        in_specs=[tile, seq, seq],
        out_specs=[tile, tile],
        out_shape=[jax.ShapeDtypeStruct((n, L, 256), F32)] * 2,
        compiler_params=_params("parallel", "arbitrary"),
        name="dilated",
    )(q, k, v)


_C_GATE, _C_MERGE, _C_NSAG = 0, 1024, 5120
_W2_COLS = 5248


def _out_kernel(x_ref, nw_ref, w2_ref, oa_ref, ocmp_ref, oslc_ref, owin_ref, oc_ref,
                od0_ref, od1_ref, od2_ref, ls0_ref, ls1_ref, ls2_ref,
                wup_ref, wout_ref, fnw_ref, spread_ref, y_ref, acc_ref, sub_ref, *, final):
    ts = x_ref.shape[1]
    x = x_ref[0]
    h = _rmsnorm(x, nw_ref[...]).astype(BF16)

    def proj(c0, w):
        return _dot(h, w2_ref[:, c0:c0 + w])

    def token_order(ref, dil, slot):
        if dil == 1:
            return ref[0, 0]
        n = ts // dil
        for c in range(dil):
            for half in range(2):
                sub_ref[slot, half, pl.ds(c, n, stride=dil), :] = ref[0, c, :, half * 128:(half + 1) * 128]
        return jnp.concatenate([sub_ref[slot, 0], sub_ref[slot, 1]], axis=1)

    gate = jax.nn.sigmoid(proj(_C_NSAG, 128))
    hi = gate.astype(BF16)
    rest = gate - hi.astype(F32)
    mid = rest.astype(BF16)
    lo = (rest - mid.astype(F32)).astype(BF16)
    spread = spread_ref[...]
    ng = _dot(hi, spread) + _dot(mid, spread) + _dot(lo, spread)
    o_b = ng[:, 0:256] * ocmp_ref[0] + ng[:, 256:512] * oslc_ref[0] + ng[:, 512:768] * owin_ref[0]

    dils = [dil for _, dil in D_PATTERNS]
    od = [token_order(r, d, i) for i, (r, d) in enumerate(zip((od0_ref, od1_ref, od2_ref), dils))]
    ls = [token_order(r, d, 3 + i) for i, (r, d) in enumerate(zip((ls0_ref, ls1_ref, ls2_ref), dils))]
    mx = jnp.maximum(jnp.maximum(ls[0], ls[1]), ls[2])
    e = [jnp.exp(l - mx) for l in ls]
    den = e[0] + e[1] + e[2]
    o_d = (e[0] / den) * od[0] + (e[1] / den) * od[1] + (e[2] / den) * od[2]

    branches = (oa_ref[0], o_b, oc_ref[0], o_d)
    wide = []
    for i in range(N_BRANCH):
        g = proj(_C_GATE + 256 * i, 256)
        wide.append((branches[i] * (g * jax.nn.sigmoid(g))).astype(BF16))
    chunk = 256
    for c0 in range(0, acc_ref.shape[1], chunk):
        y = None
        for i in range(N_BRANCH):
            u = _dot(wide[i], wup_ref[i, :, c0:c0 + chunk])
            mg = jax.nn.sigmoid(proj(_C_MERGE + 1024 * i + c0, chunk))
            y = mg * u if y is None else y + mg * u
        acc_ref[:, c0:c0 + chunk] = y
    out = x + _dot(acc_ref[...].astype(BF16), wout_ref[...])
    if final:
        out = _rmsnorm(out, fnw_ref[...])
    y_ref[0] = out


def _merge_out(x, norm_w, w2, token_outs, sub_outs, w_up, w_out, final_norm_w, final):
    b, s, d = x.shape
    ts = TOK_TILE
    tile256 = pl.BlockSpec((1, ts, 256), lambda i, j: (i, j, 0))
    sub_specs = [pl.BlockSpec((1, dil, ts // dil, 256), lambda i, j: (i, 0, j, 0)) for _, dil in D_PATTERNS]
    const = lambda a: pl.BlockSpec(a.shape, lambda i, j: (0,) * a.ndim)
    kern = functools.partial(_out_kernel, final=final)
    gate_id = np.arange(3 * BRANCH_W) // HEAD_DIM
    spread = jnp.asarray(np.arange(128)[:, None] == gate_id[None, :], dtype=BF16)
    return pl.pallas_call(
        kern,
        grid=(b, s // ts),
        in_specs=[pl.BlockSpec((1, ts, d), lambda i, j: (i, j, 0)), const(norm_w), const(w2)]
        + [tile256] * 5 + sub_specs * 2
        + [const(w_up), const(w_out), const(final_norm_w), const(spread)],
        out_specs=pl.BlockSpec((1, ts, d), lambda i, j: (i, j, 0)),
        out_shape=jax.ShapeDtypeStruct((b, s, d), F32),
        scratch_shapes=[pltpu.VMEM((ts, d), F32), pltpu.VMEM((6, 2, ts, 128), F32)],
        compiler_params=_params("parallel", "arbitrary"),
        name="merge_out",
    )(x, norm_w, w2, *token_outs, *sub_outs, w_up, w_out, final_norm_w, spread)


def _rope_tables(s):
    half = HEAD_DIM // 2
    inv = ROPE_THETA ** (-jnp.arange(half, dtype=F32) / half)
    ang = jnp.arange(s).astype(F32)[:, None] * inv[None, :]
    cos, sin = jnp.cos(ang), jnp.sin(ang)
    cos_t = jnp.tile(jnp.concatenate([cos, cos], axis=-1), (1, N_HEADS))
    sin_t = jnp.tile(jnp.concatenate([-sin, sin], axis=-1), (1, N_HEADS))
    return cos_t, sin_t


def _layer_weights(w_in, cmp_pos, cmp_w1, cmp_w2):
    o = np.concatenate([[0], np.cumsum(IN_SIZES)])
    qa, ka, va, ga, qb, kvb, gb, nsag, qkvc, gc, qkvd, gd, merge = [int(v) for v in o[:-1]]
    cols = lambda a, w: w_in[:, a:a + w]
    hd = HEAD_DIM
    w1 = jnp.concatenate([
        cols(qa, 256), cols(ka, 256), cols(va, 256), cols(qb, 256),
        cols(kvb + 2 * hd, hd), cols(kvb + 4 * hd, hd),
        cols(kvb, 2 * hd),
        cols(kvb + 3 * hd, hd), cols(kvb + 5 * hd, hd),
        cols(qkvc, 768), cols(qkvd, 2304)], axis=1)
    gate_cols = jnp.concatenate([cols(ga, 256), cols(gb, 256), cols(gc, 256), cols(gd, 256)], axis=1)
    nsag_cols = jnp.pad(cols(nsag, 3 * N_HEADS), ((0, 0), (0, 128 - 3 * N_HEADS)))
    w2 = jnp.concatenate([gate_cols, cols(merge, 4096), nsag_cols], axis=1)

    half = B_CMP_STRIDE
    pe = jnp.concatenate([cmp_pos[0], cmp_pos[1]], axis=-1)
    pe_lo = pe[:half].reshape(1, half * 128)
    pe_hi = pe[half:].reshape(1, half * 128)
    w1k = cmp_w1[0].reshape(B_CMP_LEN, hd, B_CMP_HIDDEN)
    w1v = cmp_w1[1].reshape(B_CMP_LEN, hd, B_CMP_HIDDEN)
    zw = jnp.zeros_like(w1k)
    w1_big = jnp.concatenate([jnp.concatenate([w1k, zw], axis=2),
                              jnp.concatenate([zw, w1v], axis=2)], axis=1)
    w_lo = w1_big[:half].reshape(half * 128, 2 * B_CMP_HIDDEN).astype(BF16)
    w_hi = w1_big[half:].reshape(half * 128, 2 * B_CMP_HIDDEN).astype(BF16)

    w2c = jnp.zeros((2 * B_CMP_HIDDEN, 2 * LANES), F32)
    w2c = w2c.at[:B_CMP_HIDDEN, 0:hd].set(cmp_w2[0])
    w2c = w2c.at[B_CMP_HIDDEN:, LANES:LANES + hd].set(cmp_w2[1])

    return w1, w2, pe_lo, pe_hi, w_lo, w_hi, w2c.astype(BF16)


def _overlap_t(s):
    nr = s // B_CMP_STRIDE
    nc = nr - 1
    nsel = s // B_SLC_LEN
    starts = np.arange(nr) * B_CMP_STRIDE
    j = np.arange(nsel)
    ov = ((starts[None, :] < (j[:, None] + 1) * B_SLC_LEN)
          & (starts[None, :] + B_CMP_LEN > j[:, None] * B_SLC_LEN)
          & (np.arange(nr)[None, :] < nc))
    return jnp.asarray(ov.astype(np.float32)).astype(BF16)


def _layer(x, norm_w, w_in, cmp_pos, cmp_w1, cmp_w2, w_up, w_out, final_norm_w, final, tables, ovt):
    b, s, _ = x.shape
    w1, w2, pe_lo, pe_hi, w_lo, w_hi, w2c = _layer_weights(w_in, cmp_pos, cmp_w1, cmp_w2)
    nw = norm_w.reshape(1, -1)
    (qat, ka, vat, kmean, qbt, qnt, ksa, kwp, vst, vwt, kcvc, qc, kc, vc, *qkvd) = _inproj(x, nw, w1, *tables)

    o_a = _moba(qat, ka, vat, kmean.transpose(0, 2, 1, 3))
    o_cmp, qbias = _nsa_cmp(kcvc, qbt, pe_lo, pe_hi, w_lo, w_hi, w2c, ovt)
    o_slc, o_win = _nsa_slc_win(qnt, qbias, ksa, kwp, vst, vwt)
    o_c = _stickbreak(qc, kc, vc)

    od, ls = [], []
    for g, (window, dil) in enumerate(D_PATTERNS):
        q, k, v = (a.reshape(b * dil, s // dil, 256) for a in qkvd[3 * g:3 * g + 3])
        o, l = _dilated_group(q, k, v, window // dil)
        od.append(o.reshape(b, dil, s // dil, 256))
        ls.append(l.reshape(b, dil, s // dil, 256))

    return _merge_out(x, nw, w2, (o_a, o_cmp, o_slc, o_win, o_c), (*od, *ls),
                      w_up.astype(BF16), w_out.astype(BF16), final_norm_w.reshape(1, -1), final)


def kernel(x, norm_w, w_in, nsa_cmp_pos, nsa_cmp_w1, nsa_cmp_w2, w_up, w_out, final_norm_w):
    depth = norm_w.shape[0]
    s = x.shape[1]
    tables = _rope_tables(s)
    ovt = _overlap_t(s)
    w_in = w_in.astype(BF16)
    for layer in range(depth):
        x = _layer(x, norm_w[layer], w_in[layer], nsa_cmp_pos[layer], nsa_cmp_w1[layer],
                   nsa_cmp_w2[layer], w_up[layer], w_out[layer], final_norm_w,
                   layer == depth - 1, tables, ovt)
    return x
```
